```python
import math
import jax
import jax.numpy as jnp
from jax import lax
import numpy as np

D_MODEL = 4096
BATCH = 2
SEQ = 4096
DEPTH = 2

PLE_DIM = 256
D_FF = 8192
EPS = 1e-6
ROPE_THETA = 10000.0

RET_HEADS = 8
RET_DK = 128
RET_DV = 256
RET_CHUNK = 128

GDN_HEADS = 16
GDN_DK = 128
GDN_DV = 128
GDN_CHUNK = 64
CONV_WIDTH = 5

DIFF_HEADS = 8
DIFF_D = 128
DIFF_DV = 2 * DIFF_D
Q_BLOCK = 128

N_BRANCH = 3
BRANCH_WIDTH = 2048

IN_SIZES = (
    RET_HEADS * RET_DK, RET_HEADS * RET_DK, RET_HEADS * RET_DV, RET_HEADS * RET_DV,
    GDN_HEADS * (2 * GDN_DK + GDN_DV), GDN_HEADS * GDN_DV, 2 * GDN_HEADS, 2 * GDN_HEADS,
    DIFF_HEADS * 2 * DIFF_D, DIFF_HEADS * 2 * DIFF_D, DIFF_HEADS * DIFF_DV,
    N_BRANCH * D_MODEL,
)
W_IN_COLS = sum(IN_SIZES)

kernel_name = 'hybrid_bidir_retention_gdn_diffattn_block'


def _rms(x, w=None):
    xf = x.astype(jnp.float32)
    y = xf * lax.rsqrt(jnp.mean(xf * xf, axis=-1, keepdims=True) + EPS)
    if w is not None:
        y = y * w.astype(jnp.float32)
    return y.astype(x.dtype)


def _l2norm(x):
    xf = x.astype(jnp.float32)
    return xf * lax.rsqrt(jnp.sum(xf * xf, axis=-1, keepdims=True) + EPS)


def _rope_tables(positions, dim):
    inv = 1.0 / (ROPE_THETA ** (jnp.arange(0, dim, 2, dtype=jnp.float32) / dim))
    ang = positions.astype(jnp.float32)[..., None] * inv
    ang = jnp.concatenate([ang, ang], axis=-1)[:, :, None, :]
    return jnp.cos(ang), jnp.sin(ang)


def _apply_rope(x, cos, sin):
    half = x.shape[-1] // 2
    rot = jnp.concatenate([-x[..., half:], x[..., :half]], axis=-1)
    return (x.astype(jnp.float32) * cos + rot.astype(jnp.float32) * sin).astype(x.dtype)


def _split_cols(proj):
    offs = np.cumsum(IN_SIZES)[:-1].tolist()
    return jnp.split(proj, offs, axis=-1)


def _swiglu(h, w_gate, w_up, w_down):
    return (jax.nn.silu(h @ w_gate) * (h @ w_up)) @ w_down


def _retention_one_dir(q, k, v, log_gamma, include_diag):
    b, h, s, dk = q.shape
    dv = v.shape[-1]
    c = RET_CHUNK
    n = s // c
    idx = jnp.arange(c, dtype=jnp.float32)
    dist = idx[:, None] - idx[None, :]
    mask = (dist >= 0) if include_diag else (dist > 0)
    intra_decay = jnp.where(mask, jnp.exp(log_gamma[:, None, None] * jnp.where(mask, dist, 0.0)), 0.0)
    q_decay = jnp.exp(log_gamma[:, None] * (idx + 1.0))
    k_decay = jnp.exp(log_gamma[:, None] * (c - 1.0 - idx))
    chunk_decay = jnp.exp(log_gamma * c)[None, :, None, None]
    qc = q.reshape(b, h, n, c, dk)
    kc = k.reshape(b, h, n, c, dk)
    vc = v.reshape(b, h, n, c, dv)
    scores = jnp.einsum('bhncd,bhnmd->bhncm', qc, kc) * intra_decay[None, :, None]
    intra = jnp.einsum('bhncm,bhnme->bhnce', scores, vc)
    kv = jnp.einsum('bhncd,hc,bhnce->nbhde', kc, k_decay, vc)

    def step(state, kv_n):
        return chunk_decay * state + kv_n, state

    _, prev = lax.scan(step, jnp.zeros((b, h, dk, dv), jnp.float32), kv)
    cross = jnp.einsum('bhncd,hc,nbhde->bhnce', qc, q_decay, prev)
    return (intra + cross).reshape(b, h, s, dv)


def _retention_branch(q, k, v, g, cos, sin):
    b, s, _ = q.shape
    q = _apply_rope(q.reshape(b, s, RET_HEADS, RET_DK), cos, sin) * (RET_DK ** -0.5)
    k = _apply_rope(k.reshape(b, s, RET_HEADS, RET_DK), cos, sin)
    v = v.reshape(b, s, RET_HEADS, RET_DV)
    q, k, v = (t.transpose(0, 2, 1, 3).astype(jnp.float32) for t in (q, k, v))
    log_gamma = jnp.log(1.0 - 2.0 ** (-5.0 - jnp.arange(RET_HEADS, dtype=jnp.float32)))
    fwd = _retention_one_dir(q, k, v, log_gamma, True)
    bwd = jnp.flip(_retention_one_dir(jnp.flip(q, 2), jnp.flip(k, 2), jnp.flip(v, 2), log_gamma, False), 2)
    o = _rms(fwd + bwd).transpose(0, 2, 1, 3).reshape(b, s, RET_HEADS * RET_DV)
    return o.astype(g.dtype) * jax.nn.silu(g)


def _centred_depthwise_conv(x, w):
    c = x.shape[-1]
    return lax.conv_general_dilated(
        x, w[:, None, :].astype(x.dtype), window_strides=(1,),
        padding=[(CONV_WIDTH // 2, CONV_WIDTH // 2)],
        dimension_numbers=('NWC', 'WIO', 'NWC'), feature_group_count=c)


def _gated_delta_one_dir(q, k, v, g, beta):
    b, h, s, dk = q.shape
    dv = v.shape[-1]
    c = GDN_CHUNK
    n = s // c
    qc = q.reshape(b, h, n, c, dk)
    kc = k.reshape(b, h, n, c, dk)
    vc = v.reshape(b, h, n, c, dv)
    gc = jnp.cumsum(g.reshape(b, h, n, c), axis=-1)
    bc = beta.reshape(b, h, n, c, 1)
    causal = jnp.tril(jnp.ones((c, c), dtype=bool))
    strict = jnp.tril(jnp.ones((c, c), dtype=bool), -1)
    gdiff = gc[..., :, None] - gc[..., None, :]
    decay = jnp.where(causal, jnp.exp(jnp.where(causal, gdiff, 0.0)), 0.0)
    kb = kc * bc
    a_mat = jnp.where(strict, jnp.einsum('bhnid,bhnjd->bhnij', kb, kc) * decay, 0.0)
    rhs = jnp.concatenate([vc * bc, kb * jnp.exp(gc)[..., None]], axis=-1)
    sol = lax.linalg.triangular_solve(a_mat, rhs, left_side=True, lower=True, unit_diagonal=True)
    u, w = sol[..., :dv], sol[..., dv:]
    qk = jnp.einsum('bhnid,bhnjd->bhnij', qc, kc) * decay
    q_in = qc * jnp.exp(gc)[..., None]
    k_out = kc * jnp.exp(gc[..., -1:] - gc)[..., None]
    last = jnp.exp(gc[..., -1])
    xs = tuple(jnp.moveaxis(t, 2, 0) for t in (u, w, qk, q_in, k_out, last))

    def step(state, xs_n):
        u_n, w_n, qk_n, qin_n, kout_n, last_n = xs_n
        v_new = u_n - jnp.einsum('bhck,bhkv->bhcv', w_n, state)
        o_n = jnp.einsum('bhck,bhkv->bhcv', qin_n, state) + jnp.einsum('bhij,bhjv->bhiv', qk_n, v_new)
        state = state * last_n[..., None, None] + jnp.einsum('bhck,bhcv->bhkv', kout_n, v_new)
        return state, o_n

    _, o = lax.scan(step, jnp.zeros((b, h, dk, dv), jnp.float32), xs)
    return jnp.moveaxis(o, 0, 2).reshape(b, h, s, dv)


def _gdn_branch(qkv, z, a, bt, conv_w, a_log, dt_bias, norm_w):
    b, s, _ = qkv.shape
    qkv = jax.nn.silu(_centred_depthwise_conv(qkv, conv_w))
    q, k, v = jnp.split(qkv, [GDN_HEADS * GDN_DK, 2 * GDN_HEADS * GDN_DK], axis=-1)
    q = _l2norm(q.reshape(b, s, GDN_HEADS, GDN_DK)) * (GDN_DK ** -0.5)
    k = _l2norm(k.reshape(b, s, GDN_HEADS, GDN_DK))
    v = v.reshape(b, s, GDN_HEADS, GDN_DV).astype(jnp.float32)
    q, k, v = (t.transpose(0, 2, 1, 3) for t in (q, k, v))
    a = a.reshape(b, s, 2, GDN_HEADS).astype(jnp.float32)
    bt = bt.reshape(b, s, 2, GDN_HEADS).astype(jnp.float32)
    g = -jnp.exp(a_log.astype(jnp.float32)) * jax.nn.softplus(a + dt_bias.astype(jnp.float32))
    beta = jax.nn.sigmoid(bt)
    g = g.transpose(2, 0, 3, 1)
    beta = beta.transpose(2, 0, 3, 1)
    fwd = _gated_delta_one_dir(q, k, v, g[0], beta[0])
    bwd = jnp.flip(_gated_delta_one_dir(jnp.flip(q, 2), jnp.flip(k, 2), jnp.flip(v, 2),
                                        jnp.flip(g[1], 2), jnp.flip(beta[1], 2)), 2)
    o = (fwd + bwd).transpose(0, 2, 1, 3)
    o = _rms(o, norm_w).astype(z.dtype) * jax.nn.silu(z.reshape(b, s, GDN_HEADS, GDN_DV))
    return o.reshape(b, s, GDN_HEADS * GDN_DV)


def _diff_branch(q, k, v, cos, sin, lam_params, subln_w, lambda_init):
    b, s, _ = q.shape
    q = _apply_rope(q.reshape(b, s, 2 * DIFF_HEADS, DIFF_D), cos, sin) * (DIFF_D ** -0.5)
    k = _apply_rope(k.reshape(b, s, 2 * DIFF_HEADS, DIFF_D), cos, sin)
    q = q.reshape(b, s, DIFF_HEADS, 2, DIFF_D).transpose(0, 2, 3, 1, 4)
    k = k.reshape(b, s, DIFF_HEADS, 2, DIFF_D).transpose(0, 2, 3, 1, 4)
    v = v.reshape(b, s, DIFF_HEADS, DIFF_DV).transpose(0, 2, 1, 3)
    lp = lam_params.astype(jnp.float32)
    lam = (jnp.exp(jnp.sum(lp[0] * lp[1])) - jnp.exp(jnp.sum(lp[2] * lp[3])) + lambda_init).astype(v.dtype)
    n_blk = s // Q_BLOCK
    q_blocks = jnp.moveaxis(q.reshape(b, DIFF_HEADS, 2, n_blk, Q_BLOCK, DIFF_D), 3, 0)

    def attend(qb):
        scores = jnp.einsum('bhtqd,bhtkd->bhtqk', qb, k).astype(jnp.float32)
        probs = jax.nn.softmax(scores, axis=-1).astype(v.dtype)
        o = jnp.einsum('bhtqk,bhke->bhtqe', probs, v)
        return o[:, :, 0] - lam * o[:, :, 1]

    o = lax.map(attend, q_blocks)
    o = o.transpose(1, 0, 3, 2, 4).reshape(b, s, DIFF_HEADS, DIFF_DV)
    o = _rms(o, subln_w) * (1.0 - lambda_init)
    return o.reshape(b, s, DIFF_HEADS * DIFF_DV)


def _token_mix(h, cos, sin, w_in, conv_w, a_log, dt_bias, gdn_norm_w,
               diff_lambda, diff_subln_w, w_branch, w_out, lambda_init):
    b, s, _ = h.shape
    (rq, rk, rv, rg, gqkv, gz, ga, gb, dq, dk_, dv_, gate) = _split_cols(h @ w_in)
    y_ret = _retention_branch(rq, rk, rv, rg, cos, sin)
    y_gdn = _gdn_branch(gqkv, gz, ga, gb, conv_w, a_log, dt_bias, gdn_norm_w)
    y_diff = _diff_branch(dq, dk_, dv_, cos, sin, diff_lambda, diff_subln_w, lambda_init)
    gates = jax.nn.sigmoid(gate).reshape(b, s, N_BRANCH, D_MODEL)
    merged = gates[:, :, 0] * (y_ret @ w_branch[0])
    merged = merged + gates[:, :, 1] * (y_gdn @ w_branch[1])
    merged = merged + gates[:, :, 2] * (y_diff @ w_branch[2])
    return merged @ w_out


def setup_inputs(seed: int = 0) -> dict:
    key = jax.random.key(seed)
    ks = jax.random.split(key, 24)
    f32 = jnp.float32

    def nrm(k, shape, scale):
        return jax.random.normal(k, shape, f32) * scale

    def gain(k, shape):
        return 1.0 + 0.01 * jax.random.normal(k, shape, f32)

    x = nrm(ks[0], (BATCH, SEQ, D_MODEL), 1.0)
    p = nrm(ks[1], (DEPTH, BATCH, SEQ, PLE_DIM), 1.0)
    positions = (jnp.arange(SEQ, dtype=jnp.int32)[None, :]
                 + jax.random.randint(ks[2], (BATCH, 1), 0, SEQ, dtype=jnp.int32))
    ln_ffn = gain(ks[3], (DEPTH, 2, D_MODEL))
    ffn_w_gate = nrm(ks[4], (DEPTH, 2, D_MODEL, D_FF), D_MODEL ** -0.5)
    ffn_w_up = nrm(ks[5], (DEPTH, 2, D_MODEL, D_FF), D_MODEL ** -0.5)
    ffn_w_down = nrm(ks[6], (DEPTH, 2, D_FF, D_MODEL), D_FF ** -0.5)
    ln_mix = gain(ks[7], (DEPTH, D_MODEL))
    w_in = nrm(ks[8], (DEPTH, D_MODEL, W_IN_COLS), D_MODEL ** -0.5)
    conv_w = nrm(ks[9], (DEPTH, CONV_WIDTH, GDN_HEADS * (2 * GDN_DK + GDN_DV)), CONV_WIDTH ** -0.5)
    gdn_a_log = jnp.log(jax.random.uniform(ks[10], (DEPTH, 2, GDN_HEADS), f32, 1.0, 16.0))
    dt = jnp.exp(jax.random.uniform(ks[11], (DEPTH, 2, GDN_HEADS), f32, math.log(1e-3), math.log(1e-1)))
    gdn_dt_bias = dt + jnp.log(-jnp.expm1(-dt))
    gdn_norm_w = gain(ks[12], (DEPTH, GDN_DV))
    diff_lambda = nrm(ks[13], (DEPTH, 4, DIFF_D), 0.1)
    diff_subln_w = gain(ks[14], (DEPTH, DIFF_DV))
    w_branch = nrm(ks[15], (DEPTH, N_BRANCH, BRANCH_WIDTH, D_MODEL), BRANCH_WIDTH ** -0.5)
    w_out = nrm(ks[16], (DEPTH, D_MODEL, D_MODEL), D_MODEL ** -0.5)
    ln_ple = gain(ks[17], (DEPTH, D_MODEL))
    w_ple_gate = nrm(ks[18], (DEPTH, D_MODEL, D_MODEL), D_MODEL ** -0.5)
    w_ple_proj = nrm(ks[19], (DEPTH, PLE_DIM, D_MODEL), PLE_DIM ** -0.5)
    final_norm = gain(ks[20], (D_MODEL,))
    return {'x': x, 'p': p, 'positions': positions, 'ln_ffn': ln_ffn,
            'ffn_w_gate': ffn_w_gate, 'ffn_w_up': ffn_w_up, 'ffn_w_down': ffn_w_down,
            'ln_mix': ln_mix, 'w_in': w_in, 'conv_w': conv_w, 'gdn_a_log': gdn_a_log,
            'gdn_dt_bias': gdn_dt_bias, 'gdn_norm_w': gdn_norm_w, 'diff_lambda': diff_lambda,
            'diff_subln_w': diff_subln_w, 'w_branch': w_branch, 'w_out': w_out,
            'ln_ple': ln_ple, 'w_ple_gate': w_ple_gate, 'w_ple_proj': w_ple_proj,
            'final_norm': final_norm}


def reference(x, p, positions, ln_ffn, ffn_w_gate, ffn_w_up, ffn_w_down, ln_mix, w_in,
              conv_w, gdn_a_log, gdn_dt_bias, gdn_norm_w, diff_lambda, diff_subln_w,
              w_branch, w_out, ln_ple, w_ple_gate, w_ple_proj, final_norm):
    cos, sin = _rope_tables(positions, RET_DK)
    for i in range(DEPTH):
        lambda_init = 0.8 - 0.6 * math.exp(-0.3 * i)
        h = _rms(x, ln_ffn[i, 0])
        x = x + 0.5 * _swiglu(h, ffn_w_gate[i, 0], ffn_w_up[i, 0], ffn_w_down[i, 0])
        h = _rms(x, ln_mix[i])
        x = x + _token_mix(h, cos, sin, w_in[i], conv_w[i], gdn_a_log[i], gdn_dt_bias[i],
                           gdn_norm_w[i], diff_lambda[i], diff_subln_w[i], w_branch[i],
                           w_out[i], lambda_init)
        h = _rms(x, ln_ffn[i, 1])
        x = x + 0.5 * _swiglu(h, ffn_w_gate[i, 1], ffn_w_up[i, 1], ffn_w_down[i, 1])
        h = _rms(x, ln_ple[i])
        x = x + jax.nn.sigmoid(h @ w_ple_gate[i]) * (p[i] @ w_ple_proj[i])
    return _rms(x, final_norm)
```

```python
import functools
import math

import jax
import jax.numpy as jnp
from jax import lax
from jax.experimental import pallas as pl
from jax.experimental.pallas import tpu as pltpu

F32 = jnp.float32
BF16 = jnp.bfloat16

EPS = 1e-6
ROPE_THETA = 10000.0
LANES = 128
VMEM_LIMIT = 56 * 1024 * 1024

RET_HEADS, RET_DK, RET_DV, RET_CHUNK = 8, 128, 256, 128
GDN_HEADS, GDN_DK, GDN_DV, CONV_WIDTH = 16, 128, 128, 5
GDN_CHUNK = 256
DIFF_HEADS, DIFF_D, DIFF_DV = 8, 128, 256
N_BRANCH = 3


def _params(*sem):
    return pltpu.CompilerParams(dimension_semantics=sem, vmem_limit_bytes=VMEM_LIMIT)


def _bdot(a, b):
    return jnp.dot(a.astype(BF16), b.astype(BF16), preferred_element_type=F32)


def _bdot_t(a, b, dims):
    return lax.dot_general(a.astype(BF16), b.astype(BF16), (dims, ((), ())),
                           preferred_element_type=F32)


_NT = ((1,), (1,))
_TN = ((0,), (0,))


def _fdot(a, b):
    return jnp.dot(a, b, preferred_element_type=F32, precision=lax.Precision.HIGHEST)


def _sigmoid(x):
    return 1.0 / (1.0 + jnp.exp(-x))


def _silu(x):
    return x * _sigmoid(x)


def _rms_kernel(x_ref, w_ref, o_ref):
    x = x_ref[...]
    y = x * lax.rsqrt(jnp.mean(x * x, axis=-1, keepdims=True) + EPS)
    o_ref[...] = (y * w_ref[...]).astype(o_ref.dtype)


def _rmsnorm(x, w, out_dtype, tm=256):
    m, d = x.shape
    return pl.pallas_call(
        _rms_kernel,
        grid=(m // tm,),
        in_specs=[pl.BlockSpec((tm, d), lambda i: (i, 0)),
                  pl.BlockSpec((1, d), lambda i: (0, 0))],
        out_specs=pl.BlockSpec((tm, d), lambda i: (i, 0)),
        out_shape=jax.ShapeDtypeStruct((m, d), out_dtype),
        compiler_params=_params("parallel"),
        name="rmsnorm",
    )(x, w.reshape(1, d))


def _mm_kernel(*refs, a_index, n_extra, epilogue, w_resident):
    n_a, n_w = max(a_index) + 1, len(a_index)
    a_refs = refs[:n_a]
    w_refs = refs[n_a:n_a + n_w]
    e_refs = refs[n_a + n_w:n_a + n_w + n_extra]
    o_ref = refs[n_a + n_w + n_extra]
    scratch = refs[n_a + n_w + n_extra + 1:]
    if w_resident:
        @pl.when(pl.program_id(1) == 0)
        def _():
            for w, s in zip(w_refs, scratch):
                s[...] = w[...].astype(BF16)
        ws = [s[...] for s in scratch]
    else:
        ws = [w[...].astype(BF16) for w in w_refs]
    accs = [jnp.dot(a_refs[ai][...], w, preferred_element_type=F32) for ai, w in zip(a_index, ws)]
    o_ref[...] = epilogue(accs, [e[...] for e in e_refs]).astype(o_ref.dtype)


def _matmul(pairs, extras, epilogue, n, out_dtype, tm, tn, w_resident, name):
    m = pairs[0][0].shape[0]
    if w_resident:
        grid = (n // tn, m // tm)
        ij = lambda g0, g1: (g1, g0)
        sem = ("parallel", "arbitrary")
    else:
        grid = (m // tm, n // tn)
        ij = lambda g0, g1: (g0, g1)
        sem = ("parallel", "parallel")

    a_specs, a_args, a_keys, a_index, w_specs, scratch = [], [], [], [], [], []
    for (a, acb, k, w, lead, wkb, wco) in pairs:
        key = (id(a), acb, k)
        if key not in a_keys:
            a_keys.append(key)
            a_args.append(a)
            a_specs.append(pl.BlockSpec((tm, k), lambda g0, g1, acb=acb: (ij(g0, g1)[0], acb)))
        a_index.append(a_keys.index(key))
        w_specs.append(pl.BlockSpec(
            (None,) * len(lead) + (k, tn),
            lambda g0, g1, lead=lead, wkb=wkb, wco=wco: (*lead, wkb, wco + ij(g0, g1)[1])))
        if w_resident:
            scratch.append(pltpu.VMEM((k, tn), BF16))
    e_specs = [pl.BlockSpec((tm, tn), lambda g0, g1, eco=eco: (ij(g0, g1)[0], eco + ij(g0, g1)[1]))
               for (_, eco) in extras]
    kern = functools.partial(_mm_kernel, a_index=tuple(a_index), n_extra=len(extras),
                             epilogue=epilogue, w_resident=w_resident)
    return pl.pallas_call(
        kern,
        grid=grid,
        in_specs=a_specs + w_specs + e_specs,
        out_specs=pl.BlockSpec((tm, tn), lambda g0, g1: ij(g0, g1)),
        out_shape=jax.ShapeDtypeStruct((m, n), out_dtype),
        scratch_shapes=scratch,
        compiler_params=_params(*sem),
        name=name,
    )(*a_args, *[p[3] for p in pairs], *[e[0] for e in extras])


def _ep_swiglu(accs, ex):
    return _silu(accs[0]) * accs[1]


def _ep_identity(accs, ex):
    return accs[0]


def _ep_residual(scale, accs, ex):
    return ex[0] + scale * accs[0]


def _ep_ple(accs, ex):
    return ex[0] + _sigmoid(accs[0]) * accs[1]


def _ep_merge(accs, ex):
    out = _sigmoid(ex[0]) * accs[0]
    for b in range(1, N_BRANCH):
        out = out + _sigmoid(ex[b]) * accs[b]
    return out


def _rope(x, cos, sin_signed):
    return x * cos + pltpu.roll(x, x.shape[-1] // 2, axis=x.ndim - 1) * sin_signed


def _retention_kernel(q_ref, k_ref, v_ref, g_ref, cos_ref, sin_ref, lg_ref, o_ref,
                      kr_ref, st_ref, *, seq):
    c = RET_CHUNK
    n = seq // c
    lg = lg_ref[0:1, :]
    lgk = lg[:, :RET_DK]
    ri = lax.broadcasted_iota(jnp.int32, (c, RET_DK), 0).astype(F32)
    rj = lax.broadcasted_iota(jnp.int32, (c, c), 1).astype(F32)
    q_dec_f = jnp.exp(lgk * (ri + 1.0))
    k_dec_f = jnp.exp(lgk * (c - 1.0 - ri))
    q_dec_b = jnp.exp(lgk * (c - ri))
    k_dec_b = jnp.exp(lgk * ri)
    intra_decay = jnp.exp(lgk * jnp.abs(ri[:, :c] - rj))
    chunk_decay = jnp.exp(lg * float(c))
    scale = RET_DK ** -0.5

    def rows(i):
        return pl.ds(pl.multiple_of(i * c, c), c)

    def fwd(i, state):
        r = rows(i)
        kr = _rope(k_ref[r, :], cos_ref[r, :], sin_ref[r, :])
        kr_ref[r, :] = kr
        st_ref[i] = state
        return chunk_decay * state + _bdot_t(kr * k_dec_f, v_ref[r, :], _TN)

    lax.fori_loop(0, n, fwd, jnp.zeros((RET_DK, RET_DV), F32))

    def bwd(t, state):
        i = n - 1 - t
        r = rows(i)
        q = _rope(q_ref[r, :], cos_ref[r, :], sin_ref[r, :]) * scale
        kr = kr_ref[r, :]
        v = v_ref[r, :]
        scores = _bdot_t(q, kr, _NT) * intra_decay
        o = _bdot(scores, v) + _bdot(q * q_dec_f, st_ref[i]) + _bdot(q * q_dec_b, state)
        o = o * lax.rsqrt(jnp.mean(o * o, axis=-1, keepdims=True) + EPS)
        o_ref[r, :] = (o * _silu(g_ref[r, :])).astype(o_ref.dtype)
        return chunk_decay * state + _bdot_t(kr * k_dec_b, v, _TN)

    lax.fori_loop(0, n, bwd, jnp.zeros((RET_DK, RET_DV), F32))


def _retention(p1, cos, sin_signed, batch, seq):
    lg = jnp.log(1.0 - 2.0 ** (-5.0 - jnp.arange(RET_HEADS, dtype=F32)))
    lg = jnp.broadcast_to(lg[:, None, None], (RET_HEADS, 8, RET_DV))
    kq, kv = RET_HEADS, (2 * RET_HEADS * RET_DK) // RET_DV
    return pl.pallas_call(
        functools.partial(_retention_kernel, seq=seq),
        grid=(batch, RET_HEADS),
        in_specs=[
            pl.BlockSpec((seq, RET_DK), lambda b, h: (b, h)),
            pl.BlockSpec((seq, RET_DK), lambda b, h: (b, kq + h)),
            pl.BlockSpec((seq, RET_DV), lambda b, h: (b, kv + h)),
            pl.BlockSpec((seq, RET_DV), lambda b, h: (b, kv + RET_HEADS + h)),
            pl.BlockSpec((seq, RET_DK), lambda b, h: (b, 0)),
            pl.BlockSpec((seq, RET_DK), lambda b, h: (b, 0)),
            pl.BlockSpec((None, 8, RET_DV), lambda b, h: (h, 0, 0)),
        ],
        out_specs=pl.BlockSpec((seq, RET_DV), lambda b, h: (b, h)),
        out_shape=jax.ShapeDtypeStruct((batch * seq, RET_HEADS * RET_DV), BF16),
        scratch_shapes=[pltpu.VMEM((seq, RET_DK), F32),
                        pltpu.VMEM((seq // RET_CHUNK, RET_DK, RET_DV), F32)],
        compiler_params=_params("parallel", "parallel"),
        name="retention",
    )(p1, p1, p1, p1, cos, sin_signed, lg)


def _split3(x):
    hi = x.astype(BF16)
    r = x - hi.astype(F32)
    mid = r.astype(BF16)
    lo = (r - mid.astype(F32)).astype(BF16)
    return hi, mid, lo


def _exact_dot_rhs01(x, sel):
    hi, mid, lo = _split3(x)
    d = lambda p: jnp.dot(p, sel, preferred_element_type=F32)
    return d(hi) + d(mid) + d(lo)


def _exact_dot_lhs01(sel, x):
    hi, mid, lo = _split3(x)
    d = lambda p: jnp.dot(sel, p, preferred_element_type=F32)
    return d(hi) + d(mid) + d(lo)


def _softplus(x):
    return jnp.maximum(x, 0.0) + jnp.log1p(jnp.exp(-jnp.abs(x)))


def _conv_silu_block(x_ref, w_ref, r0, rows, seq):
    mid = x_ref[pl.ds(r0, rows), :]
    top = x_ref[pl.ds(pl.multiple_of(jnp.maximum(r0 - 8, 0), 8), 8), :]
    top = jnp.where(r0 > 0, top, 0.0)
    bot = x_ref[pl.ds(pl.multiple_of(jnp.minimum(r0 + rows, seq - 8), 8), 8), :]
    bot = jnp.where(r0 + rows < seq, bot, 0.0)
    win = jnp.concatenate([top, mid, bot], axis=0)
    half = CONV_WIDTH // 2
    acc = None
    for t in range(CONV_WIDTH):
        sh = win if t == half else pltpu.roll(win, (half - t) % (rows + 16), axis=0)
        term = sh[8:8 + rows, :] * w_ref[t:t + 1, :]
        acc = term if acc is None else acc + term
    return _silu(acc)


def _l2norm(x):
    return x * lax.rsqrt(jnp.sum(x * x, axis=-1, keepdims=True) + EPS)


def _gdn_chunk(q, k, v, g, beta, state, upper):
    c = GDN_CHUNK
    ri = lax.broadcasted_iota(jnp.int32, (c, c), 0)
    ci = lax.broadcasted_iota(jnp.int32, (c, c), 1)
    incl = (ri <= ci) if upper else (ri >= ci)
    strict = (ri < ci) if upper else (ri > ci)
    eye = ri == ci
    gc = _exact_dot_lhs01(incl.astype(BF16), g)
    tot = gc[0:1, :] if upper else gc[c - 1:c, :]
    gci = jnp.concatenate([gc, gc], axis=1)
    gcj = jnp.sum(jnp.where(eye, gci, 0.0), axis=0, keepdims=True)
    decay = jnp.where(incl, jnp.exp(jnp.where(incl, gci - gcj, 0.0)), 0.0)
    egc = jnp.exp(gc)
    kb = k * beta
    a_mat = jnp.where(strict, _bdot_t(kb, k, _NT) * decay, 0.0)
    x = -a_mat
    inv = jnp.where(eye, 1.0, 0.0) + x
    pw = x
    for _ in range(int(math.log2(c)) - 1):
        pw = _fdot(pw, pw)
        inv = inv + _fdot(inv, pw)
    u = _fdot(inv, v * beta)
    w = _fdot(inv, kb * egc)
    qk = _bdot_t(q, k, _NT) * decay
    v_new = u - _bdot(w, state)
    o = _bdot(q * egc, state) + _bdot(qk, v_new)
    k_out = k * jnp.exp(tot - gc)
    state = state * jnp.exp(tot) + _bdot_t(k_out, v_new, _TN)
    return o, state


def _gdn_kernel(q_ref, k_ref, v_ref, z_ref, ab_ref, wq_ref, wk_ref, wv_ref,
                alog_ref, dt_ref, nw_ref, o_ref, qn_ref, kn_ref, vn_ref, acc_ref, *, seq):
    c = GDN_CHUNK
    n = seq // c
    h = pl.program_id(1)

    def prep(i, carry):
        r0 = pl.multiple_of(i * c, c)
        r = pl.ds(r0, c)
        qn_ref[r, :] = _l2norm(_conv_silu_block(q_ref, wq_ref, r0, c, seq)) * (GDN_DK ** -0.5)
        kn_ref[r, :] = _l2norm(_conv_silu_block(k_ref, wk_ref, r0, c, seq))
        vn_ref[r, :] = _conv_silu_block(v_ref, wv_ref, r0, c, seq)
        acc_ref[r, :] = jnp.zeros((c, GDN_DV), F32)
        return carry

    lax.fori_loop(0, n, prep, 0)

    lane = lax.broadcasted_iota(jnp.int32, (1, LANES), 1)
    sel_row = lax.broadcasted_iota(jnp.int32, (LANES, 2 * LANES), 0)
    sel_col = lax.broadcasted_iota(jnp.int32, (LANES, 2 * LANES), 1)

    def gates(r, d):
        x = ab_ref[r, :]
        gall = -jnp.exp(alog_ref[...]) * _softplus(x + dt_ref[...])
        y = jnp.where(lane < 2 * GDN_HEADS, gall, _sigmoid(x))
        g_lane = d * GDN_HEADS + h
        want = jnp.where(sel_col < LANES, g_lane, g_lane + 2 * GDN_HEADS)
        both = _exact_dot_rhs01(y, (sel_row == want).astype(BF16))
        return both[:, :LANES], both[:, LANES:]

    def step(i, states):
        sf, sb = states
        rf = pl.ds(pl.multiple_of(i * c, c), c)
        rb = pl.ds(pl.multiple_of((n - 1 - i) * c, c), c)
        gf, bf = gates(rf, 0)
        of, sf = _gdn_chunk(qn_ref[rf, :], kn_ref[rf, :], vn_ref[rf, :], gf, bf, sf, False)
        acc_ref[rf, :] += of
        gb, bb = gates(rb, 1)
        ob, sb = _gdn_chunk(qn_ref[rb, :], kn_ref[rb, :], vn_ref[rb, :], gb, bb, sb, True)
        acc_ref[rb, :] += ob
        return sf, sb

    zero = jnp.zeros((GDN_DK, GDN_DV), F32)
    lax.fori_loop(0, n, step, (zero, zero))

    def finish(i, carry):
        r = pl.ds(pl.multiple_of(i * c, c), c)
        o = acc_ref[r, :]
        o = o * lax.rsqrt(jnp.mean(o * o, axis=-1, keepdims=True) + EPS) * nw_ref[...]
        o_ref[r, :] = (o * _silu(z_ref[r, :])).astype(o_ref.dtype)
        return carry

    lax.fori_loop(0, n, finish, 0)


def _gdn(p1, pab, conv_w, a_log, dt_bias, norm_w, qkv_off, z_off, batch, seq):
    hh = GDN_HEADS
    pad = lambda t: jnp.pad(t.reshape(1, 2 * hh).astype(F32), ((0, 0), (0, LANES - 2 * hh)))
    tok = lambda off: pl.BlockSpec((seq, LANES), lambda b, h, off=off: (b, off + h))
    cw = lambda off: pl.BlockSpec((CONV_WIDTH, LANES), lambda b, h, off=off: (0, off + h))
    row = pl.BlockSpec((1, LANES), lambda b, h: (0, 0))
    return pl.pallas_call(
        functools.partial(_gdn_kernel, seq=seq),
        grid=(batch, hh),
        in_specs=[tok(qkv_off), tok(qkv_off + hh), tok(qkv_off + 2 * hh), tok(z_off),
                  pl.BlockSpec((seq, LANES), lambda b, h: (b, 0)),
                  cw(0), cw(hh), cw(2 * hh), row, row, row],
        out_specs=pl.BlockSpec((seq, GDN_DV), lambda b, h: (b, h)),
        out_shape=jax.ShapeDtypeStruct((batch * seq, hh * GDN_DV), BF16),
        scratch_shapes=[pltpu.VMEM((seq, LANES), F32)] * 4,
        compiler_params=_params("parallel", "parallel"),
        name="gdn",
    )(p1, p1, p1, p1, pab, conv_w, conv_w, conv_w, pad(a_log), pad(dt_bias),
      norm_w.reshape(1, GDN_DV).astype(F32))


def _diff_kernel(q_ref, k_ref, v_ref, cq_ref, sq_ref, ck_ref, sk_ref, lam_ref, nw_ref, o_ref,
                 kr_ref, vb_ref, *, lambda_init):
    d = DIFF_D

    @pl.when(pl.program_id(2) == 0)
    def _():
        for t in range(2):
            kr_ref[t] = _rope(k_ref[:, t * d:(t + 1) * d], ck_ref[...], sk_ref[...]).astype(BF16)
        vb_ref[...] = v_ref[...].astype(BF16)

    lp = lam_ref[...]
    lam = (jnp.exp(jnp.sum(lp[0:1] * lp[1:2], axis=-1, keepdims=True))
           - jnp.exp(jnp.sum(lp[2:3] * lp[3:4], axis=-1, keepdims=True)) + lambda_init)
    outs = []
    for t in range(2):
        q = _rope(q_ref[:, t * d:(t + 1) * d], cq_ref[...], sq_ref[...]) * (d ** -0.5)
        s = lax.dot_general(q.astype(BF16), kr_ref[t], (_NT, ((), ())),
                            preferred_element_type=F32)
        p = jnp.exp(s - jnp.max(s, axis=-1, keepdims=True))
        l = jnp.sum(p, axis=-1, keepdims=True)
        outs.append(jnp.dot(p.astype(BF16), vb_ref[...], preferred_element_type=F32) / l)
    o = outs[0] - lam * outs[1]
    o = o * lax.rsqrt(jnp.mean(o * o, axis=-1, keepdims=True) + EPS) * nw_ref[...]
    o_ref[...] = (o * (1.0 - lambda_init)).astype(o_ref.dtype)


def _diff_attention(p2, cos, sin_signed, lam_params, subln_w, lambda_init, q_off, batch, seq, tq):
    nq = seq // tq
    hh = DIFF_HEADS
    return pl.pallas_call(
        functools.partial(_diff_kernel, lambda_init=lambda_init),
        grid=(batch, hh, nq),
        in_specs=[
            pl.BlockSpec((tq, 2 * DIFF_D), lambda b, h, i: (b * nq + i, q_off + h)),
            pl.BlockSpec((seq, 2 * DIFF_D), lambda b, h, i: (b, q_off + hh + h)),
            pl.BlockSpec((seq, DIFF_DV), lambda b, h, i: (b, q_off + 2 * hh + h)),
            pl.BlockSpec((tq, DIFF_D), lambda b, h, i: (b * nq + i, 0)),
            pl.BlockSpec((tq, DIFF_D), lambda b, h, i: (b * nq + i, 0)),
            pl.BlockSpec((seq, DIFF_D), lambda b, h, i: (b, 0)),
            pl.BlockSpec((seq, DIFF_D), lambda b, h, i: (b, 0)),
            pl.BlockSpec((4, DIFF_D), lambda b, h, i: (0, 0)),
            pl.BlockSpec((1, DIFF_DV), lambda b, h, i: (0, 0)),
        ],
        out_specs=pl.BlockSpec((tq, DIFF_DV), lambda b, h, i: (b * nq + i, h)),
        out_shape=jax.ShapeDtypeStruct((batch * seq, hh * DIFF_DV), BF16),
        scratch_shapes=[pltpu.VMEM((2, seq, DIFF_D), BF16), pltpu.VMEM((seq, DIFF_DV), BF16)],
        compiler_params=_params("parallel", "parallel", "arbitrary"),
        name="diff_attention",
    )(p2, p2, p2, cos, sin_signed, cos, sin_signed, lam_params.astype(F32),
      subln_w.reshape(1, DIFF_DV).astype(F32))


def _rope_tables(positions, dim):
    inv = 1.0 / (ROPE_THETA ** (jnp.arange(0, dim, 2, dtype=F32) / dim))
    ang = positions.astype(F32)[..., None] * inv
    ang = jnp.concatenate([ang, ang], axis=-1).reshape(-1, dim)
    sign = jnp.concatenate([-jnp.ones((dim // 2,), F32), jnp.ones((dim // 2,), F32)])
    return jnp.cos(ang), jnp.sin(ang) * sign


def _ffn(x, ln, w_gate, w_up, w_down, lead, d_model, d_ff):
    h = _rmsnorm(x, ln, BF16)
    act = _matmul([(h, 0, d_model, w_gate, lead, 0, 0), (h, 0, d_model, w_up, lead, 0, 0)], [],
                  _ep_swiglu, d_ff, BF16, tm=512, tn=512, w_resident=True, name="ffn_up")
    return _matmul([(act, 0, d_ff, w_down, lead, 0, 0)], [(x, 0)],
                   functools.partial(_ep_residual, 0.5), d_model, F32,
                   tm=256, tn=512, w_resident=True, name="ffn_down")


def kernel(x, p, positions, ln_ffn, ffn_w_gate, ffn_w_up, ffn_w_down, ln_mix, w_in, conv_w,
           gdn_a_log, gdn_dt_bias, gdn_norm_w, diff_lambda, diff_subln_w, w_branch, w_out,
           ln_ple, w_ple_gate, w_ple_proj, final_norm):
    batch, seq, d_model = x.shape
    depth = p.shape[0]
    ple_dim = p.shape[-1]
    d_ff = ffn_w_gate.shape[-1]
    t = batch * seq
    cos, sin_signed = _rope_tables(positions, RET_DK)

    n_ret = RET_HEADS * (2 * RET_DK + 2 * RET_DV)
    n_gdn = GDN_HEADS * (2 * GDN_DK + 2 * GDN_DV)
    n1 = n_ret + n_gdn
    n_ab = 4 * GDN_HEADS
    n_diff = DIFF_HEADS * (4 * DIFF_D + DIFF_DV)
    n2 = n_diff + N_BRANCH * d_model
    tn_in = 512

    x = x.reshape(t, d_model)
    p = p.reshape(depth, t, ple_dim)
    for i in range(depth):
        lambda_init = 0.8 - 0.6 * math.exp(-0.3 * i)
        x = _ffn(x, ln_ffn[i, 0], ffn_w_gate, ffn_w_up, ffn_w_down, (i, 0), d_model, d_ff)

        h = _rmsnorm(x, ln_mix[i], BF16)
        p1 = _matmul([(h, 0, d_model, w_in, (i,), 0, 0)], [], _ep_identity, n1, F32,
                     tm=1024, tn=tn_in, w_resident=True, name="w_in_head")
        pab = _matmul([(h, 0, d_model, w_in, (i,), 0, n1 // LANES)], [], _ep_identity, LANES, F32,
                      tm=1024, tn=LANES, w_resident=True, name="w_in_gates")
        w_tail = w_in[i, :, n1 + n_ab:]
        p2 = _matmul([(h, 0, d_model, w_tail, (), 0, 0)], [], _ep_identity, n2, F32,
                     tm=1024, tn=tn_in, w_resident=True, name="w_in_tail")

        y_ret = _retention(p1, cos, sin_signed, batch, seq)
        y_gdn = _gdn(p1, pab, conv_w[i], gdn_a_log[i], gdn_dt_bias[i], gdn_norm_w[i],
                     n_ret // LANES, (n_ret + GDN_HEADS * (2 * GDN_DK + GDN_DV)) // LANES,
                     batch, seq)
        y_diff = _diff_attention(p2, cos, sin_signed, diff_lambda[i], diff_subln_w[i],
                                 lambda_init, 0, batch, seq, tq=256)

        bw = RET_HEADS * RET_DV
        tn_m = 512
        gate_off = n_diff // tn_m
        merged = _matmul(
            [(y, 0, bw, w_branch, (i, b), 0, 0) for b, y in enumerate((y_ret, y_gdn, y_diff))],
            [(p2, gate_off + b * (d_model // tn_m)) for b in range(N_BRANCH)],
            _ep_merge, d_model, BF16, tm=512, tn=tn_m, w_resident=True, name="merge")
        x = _matmul([(merged, 0, d_model, w_out, (i,), 0, 0)], [(x, 0)],
                    functools.partial(_ep_residual, 1.0), d_model, F32,
                    tm=512, tn=512, w_resident=True, name="w_out")

        x = _ffn(x, ln_ffn[i, 1], ffn_w_gate, ffn_w_up, ffn_w_down, (i, 1), d_model, d_ff)

        h = _rmsnorm(x, ln_ple[i], BF16)
        x = _matmul([(h, 0, d_model, w_ple_gate, (i,), 0, 0),
                     (p[i].astype(BF16), 0, ple_dim, w_ple_proj, (i,), 0, 0)], [(x, 0)],
                    _ep_ple, d_model, F32, tm=512, tn=512, w_resident=True, name="ple")
    return _rmsnorm(x, final_norm, F32).reshape(batch, seq, d_model)
```

```python
import functools
import math

import jax
import jax.numpy as jnp
from jax import lax
from jax.experimental import pallas as pl
from jax.experimental.pallas import tpu as pltpu

F32 = jnp.float32
BF16 = jnp.bfloat16

EPS = 1e-6
ROPE_THETA = 10000.0
LOG2_E = math.log2(math.e)
LANES = 128
VMEM_LIMIT = 56 * 1024 * 1024

RET_HEADS, RET_DK, RET_DV, RET_CHUNK = 8, 128, 256, 128
GDN_HEADS, GDN_DK, GDN_DV, CONV_WIDTH = 16, 128, 128, 5
GDN_CHUNK = 256
GDN_UNROLL = 4
DIFF_HEADS, DIFF_D, DIFF_DV = 8, 128, 256
N_BRANCH = 3


def _params(*sem):
    return pltpu.CompilerParams(dimension_semantics=sem, vmem_limit_bytes=VMEM_LIMIT)


def _bdot(a, b):
    return jnp.dot(a.astype(BF16), b.astype(BF16), preferred_element_type=F32)


def _bdot_t(a, b, dims):
    return lax.dot_general(a.astype(BF16), b.astype(BF16), (dims, ((), ())),
                           preferred_element_type=F32)


_NT = ((1,), (1,))
_TN = ((0,), (0,))


def _fdot(a, b):
    return jnp.dot(a, b, preferred_element_type=F32, precision=lax.Precision.HIGHEST)


def _sigmoid(x):
    return 1.0 / (1.0 + jnp.exp(-x))


def _silu(x):
    return x * _sigmoid(x)


def _rms_kernel(x_ref, w_ref, o_ref):
    x = x_ref[...]
    y = x * lax.rsqrt(jnp.mean(x * x, axis=-1, keepdims=True) + EPS)
    o_ref[...] = (y * w_ref[...]).astype(o_ref.dtype)


def _rmsnorm(x, w, out_dtype, tm=256):
    m, d = x.shape
    return pl.pallas_call(
        _rms_kernel,
        grid=(m // tm,),
        in_specs=[pl.BlockSpec((tm, d), lambda i: (i, 0)),
                  pl.BlockSpec((1, d), lambda i: (0, 0))],
        out_specs=pl.BlockSpec((tm, d), lambda i: (i, 0)),
        out_shape=jax.ShapeDtypeStruct((m, d), out_dtype),
        compiler_params=_params("parallel"),
        name="rmsnorm",
    )(x, w.reshape(1, d))


CAST_ROWS = 512


def _mm_kernel(*refs, a_index, n_extra, epilogue, lane_shift):
    n_a, n_w = max(a_index) + 1, len(a_index)
    n_wrefs = n_w * (2 if lane_shift else 1)
    a_refs = refs[:n_a]
    w_refs = refs[n_a:n_a + n_wrefs]
    e_refs = refs[n_a + n_wrefs:n_a + n_wrefs + n_extra]
    o_ref = refs[n_a + n_wrefs + n_extra]
    scratch = refs[n_a + n_wrefs + n_extra + 1:]

    @pl.when(pl.program_id(1) == 0)
    def _():
        for p, s in enumerate(scratch):
            k, tn = s.shape
            for r0 in range(0, k, CAST_ROWS):
                rows = slice(r0, min(r0 + CAST_ROWS, k))
                if lane_shift:
                    w = jnp.concatenate([w_refs[2 * p][rows, :], w_refs[2 * p + 1][rows, :]], axis=1)
                    w = w[:, lane_shift:lane_shift + tn]
                else:
                    w = w_refs[p][rows, :]
                s[rows, :] = w.astype(BF16)

    accs = [jnp.dot(a_refs[ai][...], s[...], preferred_element_type=F32)
            for ai, s in zip(a_index, scratch)]
    o_ref[...] = epilogue(accs, [e[...] for e in e_refs]).astype(o_ref.dtype)


def _matmul(pairs, extras, epilogue, n, out_dtype, tm, tn, name, lane_shift=0):
    m = pairs[0][0].shape[0]
    per = tn // LANES
    a_specs, a_args, a_keys, a_index, w_specs, w_args, scratch = [], [], [], [], [], [], []
    for (a, acb, k, w, lead, wkb, wco) in pairs:
        key = (id(a), acb, k)
        if key not in a_keys:
            a_keys.append(key)
            a_args.append(a)
            a_specs.append(pl.BlockSpec((tm, k), lambda j, i, acb=acb: (i, acb)))
        a_index.append(a_keys.index(key))
        squeeze = (None,) * len(lead)
        w_specs.append(pl.BlockSpec(squeeze + (k, tn),
                                    lambda j, i, lead=lead, wkb=wkb, wco=wco: (*lead, wkb, wco + j)))
        w_args.append(w)
        if lane_shift:
            w_specs.append(pl.BlockSpec(
                squeeze + (k, LANES),
                lambda j, i, lead=lead, wkb=wkb, wco=wco: (*lead, wkb, (wco + j + 1) * per)))
            w_args.append(w)
        scratch.append(pltpu.VMEM((k, tn), BF16))
    e_specs = [pl.BlockSpec((tm, tn), lambda j, i, eco=eco: (i, eco + j)) for (_, eco) in extras]
    kern = functools.partial(_mm_kernel, a_index=tuple(a_index), n_extra=len(extras),
                             epilogue=epilogue, lane_shift=lane_shift)
    return pl.pallas_call(
        kern,
        grid=(n // tn, m // tm),
        in_specs=a_specs + w_specs + e_specs,
        out_specs=pl.BlockSpec((tm, tn), lambda j, i: (i, j)),
        out_shape=jax.ShapeDtypeStruct((m, n), out_dtype),
        scratch_shapes=scratch,
        compiler_params=_params("parallel", "arbitrary"),
        name=name,
    )(*a_args, *w_args, *[e[0] for e in extras])


def _ep_swiglu(accs, ex):
    return _silu(accs[0]) * accs[1]


def _ep_identity(accs, ex):
    return accs[0]


def _ep_residual(scale, accs, ex):
    return ex[0] + scale * accs[0]


def _ep_ple(accs, ex):
    return ex[0] + _sigmoid(accs[0]) * accs[1]


def _ep_merge(accs, ex):
    out = _sigmoid(ex[0]) * accs[0]
    for b in range(1, N_BRANCH):
        out = out + _sigmoid(ex[b]) * accs[b]
    return out


def _rope(x, cos, sin_signed):
    return x * cos + pltpu.roll(x, x.shape[-1] // 2, axis=x.ndim - 1) * sin_signed


def _retention_kernel(q_ref, k_ref, v_ref, g_ref, cos_ref, sin_ref, lg_ref, o_ref,
                      kr_ref, st_ref, *, seq):
    c = RET_CHUNK
    n = seq // c
    lg = lg_ref[0:1, :]
    lgk = lg[:, :RET_DK]
    ri = lax.broadcasted_iota(jnp.int32, (c, RET_DK), 0).astype(F32)
    rj = lax.broadcasted_iota(jnp.int32, (c, c), 1).astype(F32)
    q_dec_f = jnp.exp(lgk * (ri + 1.0))
    k_dec_f = jnp.exp(lgk * (c - 1.0 - ri))
    q_dec_b = jnp.exp(lgk * (c - ri))
    k_dec_b = jnp.exp(lgk * ri)
    intra_decay = jnp.exp(lgk * jnp.abs(ri[:, :c] - rj))
    chunk_decay = jnp.exp(lg * float(c))
    scale = RET_DK ** -0.5

    def rows(i):
        return pl.ds(pl.multiple_of(i * c, c), c)

    def fwd(i, state):
        r = rows(i)
        kr = _rope(k_ref[r, :], cos_ref[r, :], sin_ref[r, :])
        kr_ref[r, :] = kr
        st_ref[i] = state
        return chunk_decay * state + _bdot_t(kr * k_dec_f, v_ref[r, :], _TN)

    lax.fori_loop(0, n, fwd, jnp.zeros((RET_DK, RET_DV), F32))

    def bwd(t, state):
        i = n - 1 - t
        r = rows(i)
        q = _rope(q_ref[r, :], cos_ref[r, :], sin_ref[r, :]) * scale
        kr = kr_ref[r, :]
        v = v_ref[r, :]
        scores = _bdot_t(q, kr, _NT) * intra_decay
        o = _bdot(scores, v) + _bdot(q * q_dec_f, st_ref[i]) + _bdot(q * q_dec_b, state)
        o = o * lax.rsqrt(jnp.mean(o * o, axis=-1, keepdims=True) + EPS)
        o_ref[r, :] = (o * _silu(g_ref[r, :])).astype(o_ref.dtype)
        return chunk_decay * state + _bdot_t(kr * k_dec_b, v, _TN)

    lax.fori_loop(0, n, bwd, jnp.zeros((RET_DK, RET_DV), F32))


def _retention(p1, cos, sin_signed, batch, seq):
    lg = jnp.log(1.0 - 2.0 ** (-5.0 - jnp.arange(RET_HEADS, dtype=F32)))
    lg = jnp.broadcast_to(lg[:, None, None], (RET_HEADS, 8, RET_DV))
    kq, kv = RET_HEADS, (2 * RET_HEADS * RET_DK) // RET_DV
    return pl.pallas_call(
        functools.partial(_retention_kernel, seq=seq),
        grid=(batch, RET_HEADS),
        in_specs=[
            pl.BlockSpec((seq, RET_DK), lambda b, h: (b, h)),
            pl.BlockSpec((seq, RET_DK), lambda b, h: (b, kq + h)),
            pl.BlockSpec((seq, RET_DV), lambda b, h: (b, kv + h)),
            pl.BlockSpec((seq, RET_DV), lambda b, h: (b, kv + RET_HEADS + h)),
            pl.BlockSpec((seq, RET_DK), lambda b, h: (b, 0)),
            pl.BlockSpec((seq, RET_DK), lambda b, h: (b, 0)),
            pl.BlockSpec((None, 8, RET_DV), lambda b, h: (h, 0, 0)),
        ],
        out_specs=pl.BlockSpec((seq, RET_DV), lambda b, h: (b, h)),
        out_shape=jax.ShapeDtypeStruct((batch * seq, RET_HEADS * RET_DV), BF16),
        scratch_shapes=[pltpu.VMEM((seq, RET_DK), F32),
                        pltpu.VMEM((seq // RET_CHUNK, RET_DK, RET_DV), F32)],
        compiler_params=_params("parallel", "parallel"),
        name="retention",
    )(p1, p1, p1, p1, cos, sin_signed, lg)


def _split2(x):
    hi = x.astype(BF16)
    lo = (x - hi.astype(F32)).astype(BF16)
    return hi, lo


def _wide_dot_rhs01(x, sel):
    hi, lo = _split2(x)
    return jnp.dot(jnp.concatenate([hi, lo], axis=1), jnp.concatenate([sel, sel], axis=0),
                   preferred_element_type=F32)


def _wide_dot_lhs01(sel, x):
    hi, lo = _split2(x)
    r = jnp.dot(sel, jnp.concatenate([hi, lo], axis=1), preferred_element_type=F32)
    return r[:, :LANES] + r[:, LANES:]


def _softplus(x):
    return jnp.maximum(x, 0.0) + jnp.log1p(jnp.exp(-jnp.abs(x)))


def _conv_silu_block(x_ref, w_ref, r0, rows, seq):
    mid = x_ref[pl.ds(r0, rows), :]
    top = x_ref[pl.ds(pl.multiple_of(jnp.maximum(r0 - 8, 0), 8), 8), :]
    top = jnp.where(r0 > 0, top, 0.0)
    bot = x_ref[pl.ds(pl.multiple_of(jnp.minimum(r0 + rows, seq - 8), 8), 8), :]
    bot = jnp.where(r0 + rows < seq, bot, 0.0)
    win = jnp.concatenate([top, mid, bot], axis=0)
    half = CONV_WIDTH // 2
    acc = None
    for t in range(CONV_WIDTH):
        sh = win if t == half else pltpu.roll(win, (half - t) % (rows + 16), axis=0)
        term = sh[8:8 + rows, :] * w_ref[t:t + 1, :]
        acc = term if acc is None else acc + term
    return _silu(acc)


def _l2norm(x):
    return x * lax.rsqrt(jnp.sum(x * x, axis=-1, keepdims=True) + EPS)


def _gdn_prepare(chunks):
    c = GDN_CHUNK
    ri = lax.broadcasted_iota(jnp.int32, (c, c), 0)
    ci = lax.broadcasted_iota(jnp.int32, (c, c), 1)
    eye = ri == ci
    incl = {False: ri >= ci, True: ri <= ci}
    strict = {False: ri > ci, True: ri < ci}
    incl_bf = {up: m.astype(BF16) for up, m in incl.items()}
    ups = [ch[5] for ch in chunks]
    gcs = [_wide_dot_lhs01(incl_bf[up], ch[3]) for ch, up in zip(chunks, ups)]
    tots = [gc[0:1, :] if up else gc[c - 1:c, :] for gc, up in zip(gcs, ups)]
    decays = []
    for gc, up in zip(gcs, ups):
        gci = jnp.concatenate([gc, gc], axis=1)
        gcj = jnp.sum(jnp.where(eye, gci, 0.0), axis=0, keepdims=True)
        decays.append(jnp.where(incl[up], jnp.exp(jnp.where(incl[up], gci - gcj, 0.0)), 0.0))
    egcs = [jnp.exp(gc) for gc in gcs]
    kbs = [ch[1] * ch[4] for ch in chunks]
    a_mats = [jnp.where(strict[up], _bdot_t(kb, ch[1], _NT) * dec, 0.0)
              for kb, ch, dec, up in zip(kbs, chunks, decays, ups)]
    qks = [_bdot_t(ch[0], ch[1], _NT) * dec for ch, dec in zip(chunks, decays)]
    invs = _unit_triangular_inverse(a_mats, ri, ci, eye)
    sols = [_bdot(inv, jnp.concatenate([ch[2] * ch[4], kb * egc], axis=1))
            for inv, ch, kb, egc in zip(invs, chunks, kbs, egcs)]
    return [(sol[:, :GDN_DV], sol[:, GDN_DV:], qk, ch[0] * egc, ch[1] * jnp.exp(tot - gc), jnp.exp(tot))
            for sol, qk, ch, egc, tot, gc in zip(sols, qks, chunks, egcs, tots, gcs)]


def _unit_triangular_inverse(a_mats, ri, ci, eye):
    c = a_mats[0].shape[0]
    base = 16
    same = (ri // base) == (ci // base)
    pws = [jnp.where(same, -a, 0.0) for a in a_mats]
    invs = [jnp.where(eye, 1.0, 0.0) + x for x in pws]
    for _ in range(int(math.log2(base)) - 1):
        pws = [_bdot(pw, pw) for pw in pws]
        invs = [inv + _bdot(inv, pw) for inv, pw in zip(invs, pws)]
    size = base
    while size < c:
        wider = (ri // (2 * size)) == (ci // (2 * size))
        pick = wider & jnp.logical_not(same)
        inner = [_bdot(jnp.where(pick, a, 0.0), inv) for a, inv in zip(a_mats, invs)]
        invs = [inv - _bdot(inv, e) for inv, e in zip(invs, inner)]
        same, size = wider, 2 * size
    return invs


def _gdn_chunk_step(pre, state):
    u, w, qk, q_in, k_out, e_tot = pre
    v_new = u - _bdot(w, state)
    o = _bdot(q_in, state) + _bdot(qk, v_new)
    state = state * e_tot + _bdot_t(k_out, v_new, _TN)
    return o, state


def _gdn_kernel(q_ref, k_ref, v_ref, z_ref, ab_ref, wq_ref, wk_ref, wv_ref,
                alog_ref, dt_ref, nw_ref, o_ref, qn_ref, kn_ref, vn_ref, acc_ref, *, seq):
    c = GDN_CHUNK
    n = seq // c
    h = pl.program_id(1)

    def prep(i, carry):
        r0 = pl.multiple_of(i * c, c)
        r = pl.ds(r0, c)
        qn_ref[r, :] = _l2norm(_conv_silu_block(q_ref, wq_ref, r0, c, seq)) * (GDN_DK ** -0.5)
        kn_ref[r, :] = _l2norm(_conv_silu_block(k_ref, wk_ref, r0, c, seq))
        vn_ref[r, :] = _conv_silu_block(v_ref, wv_ref, r0, c, seq)
        acc_ref[r, :] = jnp.zeros((c, GDN_DV), F32)
        return carry

    lax.fori_loop(0, n, prep, 0)

    lane = lax.broadcasted_iota(jnp.int32, (1, LANES), 1)
    sel_row = lax.broadcasted_iota(jnp.int32, (LANES, 2 * LANES), 0)
    sel_col = lax.broadcasted_iota(jnp.int32, (LANES, 2 * LANES), 1)

    def gates(r, d):
        x = ab_ref[r, :]
        gall = -jnp.exp(alog_ref[...]) * _softplus(x + dt_ref[...])
        y = jnp.where(lane < 2 * GDN_HEADS, gall, _sigmoid(x))
        g_lane = d * GDN_HEADS + h
        want = jnp.where(sel_col < LANES, g_lane, g_lane + 2 * GDN_HEADS)
        both = _wide_dot_rhs01(y, (sel_row == want).astype(BF16))
        return both[:, :LANES], both[:, LANES:]

    def step(i, states):
        sf, sb = states
        which = []
        for t in range(GDN_UNROLL):
            which += [(i * GDN_UNROLL + t, 0), (n - 1 - i * GDN_UNROLL - t, 1)]
        rows = [pl.ds(pl.multiple_of(j * c, c), c) for j, _ in which]
        gb = [gates(r, d) for r, (_, d) in zip(rows, which)]
        pre = _gdn_prepare([(qn_ref[r, :], kn_ref[r, :], vn_ref[r, :], g, beta, d == 1)
                            for r, (g, beta), (_, d) in zip(rows, gb, which)])
        outs = []
        for t in range(GDN_UNROLL):
            of, sf = _gdn_chunk_step(pre[2 * t], sf)
            ob, sb = _gdn_chunk_step(pre[2 * t + 1], sb)
            outs += [of, ob]
        for r, o in zip(rows, outs):
            acc_ref[r, :] += o
        return sf, sb

    zero = jnp.zeros((GDN_DK, GDN_DV), F32)
    lax.fori_loop(0, n // GDN_UNROLL, step, (zero, zero))

    def finish(i, carry):
        r = pl.ds(pl.multiple_of(i * c, c), c)
        o = acc_ref[r, :]
        o = o * lax.rsqrt(jnp.mean(o * o, axis=-1, keepdims=True) + EPS) * nw_ref[...]
        o_ref[r, :] = (o * _silu(z_ref[r, :])).astype(o_ref.dtype)
        return carry

    lax.fori_loop(0, n, finish, 0)


def _gdn(p1, pab, conv_w, a_log, dt_bias, norm_w, qkv_off, z_off, batch, seq):
    hh = GDN_HEADS
    pad = lambda t: jnp.pad(t.reshape(1, 2 * hh).astype(F32), ((0, 0), (0, LANES - 2 * hh)))
    tok = lambda off: pl.BlockSpec((seq, LANES), lambda b, h, off=off: (b, off + h))
    cw = lambda off: pl.BlockSpec((CONV_WIDTH, LANES), lambda b, h, off=off: (0, off + h))
    row = pl.BlockSpec((1, LANES), lambda b, h: (0, 0))
    return pl.pallas_call(
        functools.partial(_gdn_kernel, seq=seq),
        grid=(batch, hh),
        in_specs=[tok(qkv_off), tok(qkv_off + hh), tok(qkv_off + 2 * hh), tok(z_off),
                  pl.BlockSpec((seq, LANES), lambda b, h: (b, 0)),
                  cw(0), cw(hh), cw(2 * hh), row, row, row],
        out_specs=pl.BlockSpec((seq, GDN_DV), lambda b, h: (b, h)),
        out_shape=jax.ShapeDtypeStruct((batch * seq, hh * GDN_DV), BF16),
        scratch_shapes=[pltpu.VMEM((seq, LANES), F32)] * 4,
        compiler_params=_params("parallel", "parallel"),
        name="gdn",
    )(p1, p1, p1, p1, pab, conv_w, conv_w, conv_w, pad(a_log), pad(dt_bias),
      norm_w.reshape(1, GDN_DV).astype(F32))


def _diff_kernel(q_ref, k_ref, v_ref, cq_ref, sq_ref, ck_ref, sk_ref, lam_ref, nw_ref, o_ref,
                 kr_ref, vb_ref, *, lambda_init):
    d = DIFF_D

    @pl.when(pl.program_id(2) == 0)
    def _():
        for t in range(2):
            kr_ref[t] = _rope(k_ref[:, t * d:(t + 1) * d], ck_ref[...], sk_ref[...]).astype(BF16)
        vb_ref[...] = v_ref[...].astype(BF16)

    lp = lam_ref[...]
    lam = (jnp.exp(jnp.sum(lp[0:1] * lp[1:2], axis=-1, keepdims=True))
           - jnp.exp(jnp.sum(lp[2:3] * lp[3:4], axis=-1, keepdims=True)) + lambda_init)
    outs = []
    for t in range(2):
        q = _rope(q_ref[:, t * d:(t + 1) * d], cq_ref[...], sq_ref[...]) * (d ** -0.5 * LOG2_E)
        s = lax.dot_general(q.astype(BF16), kr_ref[t], (_NT, ((), ())),
                            preferred_element_type=F32)
        p = jnp.exp2(s - jnp.max(s, axis=-1, keepdims=True))
        l = jnp.sum(p, axis=-1, keepdims=True)
        outs.append(jnp.dot(p.astype(BF16), vb_ref[...], preferred_element_type=F32) / l)
    o = outs[0] - lam * outs[1]
    o = o * lax.rsqrt(jnp.mean(o * o, axis=-1, keepdims=True) + EPS) * nw_ref[...]
    o_ref[...] = (o * (1.0 - lambda_init)).astype(o_ref.dtype)


def _diff_attention(p2, cos, sin_signed, lam_params, subln_w, lambda_init, q_off, batch, seq, tq):
    nq = seq // tq
    hh = DIFF_HEADS
    return pl.pallas_call(
        functools.partial(_diff_kernel, lambda_init=lambda_init),
        grid=(batch, hh, nq),
        in_specs=[
            pl.BlockSpec((tq, 2 * DIFF_D), lambda b, h, i: (b * nq + i, q_off + h)),
            pl.BlockSpec((seq, 2 * DIFF_D), lambda b, h, i: (b, q_off + hh + h)),
            pl.BlockSpec((seq, DIFF_DV), lambda b, h, i: (b, q_off + 2 * hh + h)),
            pl.BlockSpec((tq, DIFF_D), lambda b, h, i: (b * nq + i, 0)),
            pl.BlockSpec((tq, DIFF_D), lambda b, h, i: (b * nq + i, 0)),
            pl.BlockSpec((seq, DIFF_D), lambda b, h, i: (b, 0)),
            pl.BlockSpec((seq, DIFF_D), lambda b, h, i: (b, 0)),
            pl.BlockSpec((4, DIFF_D), lambda b, h, i: (0, 0)),
            pl.BlockSpec((1, DIFF_DV), lambda b, h, i: (0, 0)),
        ],
        out_specs=pl.BlockSpec((tq, DIFF_DV), lambda b, h, i: (b * nq + i, h)),
        out_shape=jax.ShapeDtypeStruct((batch * seq, hh * DIFF_DV), BF16),
        scratch_shapes=[pltpu.VMEM((2, seq, DIFF_D), BF16), pltpu.VMEM((seq, DIFF_DV), BF16)],
        compiler_params=_params("parallel", "parallel", "arbitrary"),
        name="diff_attention",
    )(p2, p2, p2, cos, sin_signed, cos, sin_signed, lam_params.astype(F32),
      subln_w.reshape(1, DIFF_DV).astype(F32))


def _rope_tables(positions, dim):
    inv = 1.0 / (ROPE_THETA ** (jnp.arange(0, dim, 2, dtype=F32) / dim))
    ang = positions.astype(F32)[..., None] * inv
    ang = jnp.concatenate([ang, ang], axis=-1).reshape(-1, dim)
    sign = jnp.concatenate([-jnp.ones((dim // 2,), F32), jnp.ones((dim // 2,), F32)])
    return jnp.cos(ang), jnp.sin(ang) * sign


def _ffn(x, ln, w_gate, w_up, w_down, lead, d_model, d_ff):
    h = _rmsnorm(x, ln, BF16)
    act = _matmul([(h, 0, d_model, w_gate, lead, 0, 0), (h, 0, d_model, w_up, lead, 0, 0)], [],
                  _ep_swiglu, d_ff, BF16, tm=512, tn=512, name="ffn_up")
    return _matmul([(act, 0, d_ff, w_down, lead, 0, 0)], [(x, 0)],
                   functools.partial(_ep_residual, 0.5), d_model, F32,
                   tm=256, tn=512, name="ffn_down")


def kernel(x, p, positions, ln_ffn, ffn_w_gate, ffn_w_up, ffn_w_down, ln_mix, w_in, conv_w,
           gdn_a_log, gdn_dt_bias, gdn_norm_w, diff_lambda, diff_subln_w, w_branch, w_out,
           ln_ple, w_ple_gate, w_ple_proj, final_norm):
    batch, seq, d_model = x.shape
    depth = p.shape[0]
    ple_dim = p.shape[-1]
    d_ff = ffn_w_gate.shape[-1]
    t = batch * seq
    cos, sin_signed = _rope_tables(positions, RET_DK)

    n_ret = RET_HEADS * (2 * RET_DK + 2 * RET_DV)
    n_gdn = GDN_HEADS * (2 * GDN_DK + 2 * GDN_DV)
    n1 = n_ret + n_gdn
    n_ab = 4 * GDN_HEADS
    n_diff = DIFF_HEADS * (4 * DIFF_D + DIFF_DV)
    n2 = n_diff + N_BRANCH * d_model
    tn_in = 512

    x = x.reshape(t, d_model)
    p = p.reshape(depth, t, ple_dim)
    for i in range(depth):
        lambda_init = 0.8 - 0.6 * math.exp(-0.3 * i)
        x = _ffn(x, ln_ffn[i, 0], ffn_w_gate, ffn_w_up, ffn_w_down, (i, 0), d_model, d_ff)

        h = _rmsnorm(x, ln_mix[i], BF16)
        p1 = _matmul([(h, 0, d_model, w_in, (i,), 0, 0)], [], _ep_identity, n1, F32,
                     tm=1024, tn=tn_in, name="w_in_head")
        pab = _matmul([(h, 0, d_model, w_in, (i,), 0, n1 // LANES)], [], _ep_identity, LANES, F32,
                      tm=1024, tn=LANES, name="w_in_gates")
        p2 = _matmul([(h, 0, d_model, w_in, (i,), 0, n1 // tn_in)], [], _ep_identity, n2, F32,
                     tm=1024, tn=tn_in, name="w_in_tail", lane_shift=n_ab)

        y_ret = _retention(p1, cos, sin_signed, batch, seq)
        y_gdn = _gdn(p1, pab, conv_w[i], gdn_a_log[i], gdn_dt_bias[i], gdn_norm_w[i],
                     n_ret // LANES, (n_ret + GDN_HEADS * (2 * GDN_DK + GDN_DV)) // LANES,
                     batch, seq)
        y_diff = _diff_attention(p2, cos, sin_signed, diff_lambda[i], diff_subln_w[i],
                                 lambda_init, 0, batch, seq, tq=256)

        bw = RET_HEADS * RET_DV
        tn_m = 512
        gate_off = n_diff // tn_m
        merged = _matmul(
            [(y, 0, bw, w_branch, (i, b), 0, 0) for b, y in enumerate((y_ret, y_gdn, y_diff))],
            [(p2, gate_off + b * (d_model // tn_m)) for b in range(N_BRANCH)],
            _ep_merge, d_model, BF16, tm=512, tn=tn_m, name="merge")
        x = _matmul([(merged, 0, d_model, w_out, (i,), 0, 0)], [(x, 0)],
                    functools.partial(_ep_residual, 1.0), d_model, F32,
                    tm=512, tn=512, name="w_out")

        x = _ffn(x, ln_ffn[i, 1], ffn_w_gate, ffn_w_up, ffn_w_down, (i, 1), d_model, d_ff)

        h = _rmsnorm(x, ln_ple[i], BF16)
        x = _matmul([(h, 0, d_model, w_ple_gate, (i,), 0, 0),
                     (p[i].astype(BF16), 0, ple_dim, w_ple_proj, (i,), 0, 0)], [(x, 0)],
                    _ep_ple, d_model, F32, tm=512, tn=512, name="ple")
    return _rmsnorm(x, final_norm, F32).reshape(batch, seq, d_model)
```

```python
import functools
import math

import jax
import jax.numpy as jnp
from jax import lax
from jax.experimental import pallas as pl
from jax.experimental.pallas import tpu as pltpu

F32 = jnp.float32
BF16 = jnp.bfloat16

EPS = 1e-6
ROPE_THETA = 10000.0
LOG2_E = math.log2(math.e)
LANES = 128
VMEM_LIMIT = 56 * 1024 * 1024

RET_HEADS, RET_DK, RET_DV, RET_CHUNK = 8, 128, 256, 128
GDN_HEADS, GDN_DK, GDN_DV, CONV_WIDTH = 16, 128, 128, 5
GDN_CHUNK = 256
GDN_UNROLL = 4
DIFF_HEADS, DIFF_D, DIFF_DV = 8, 128, 256
N_BRANCH = 3


def _params(*sem):
    return pltpu.CompilerParams(dimension_semantics=sem, vmem_limit_bytes=VMEM_LIMIT)


def _bdot(a, b):
    return jnp.dot(a.astype(BF16), b.astype(BF16), preferred_element_type=F32)


def _bdot_t(a, b, dims):
    return lax.dot_general(a.astype(BF16), b.astype(BF16), (dims, ((), ())),
                           preferred_element_type=F32)


_NT = ((1,), (1,))
_TN = ((0,), (0,))


def _fdot(a, b):
    return jnp.dot(a, b, preferred_element_type=F32, precision=lax.Precision.HIGHEST)


def _sigmoid(x):
    return 1.0 / (1.0 + jnp.exp(-x))


def _silu(x):
    return x * _sigmoid(x)


def _rms_kernel(x_ref, w_ref, o_ref):
    x = x_ref[...]
    y = x * lax.rsqrt(jnp.mean(x * x, axis=-1, keepdims=True) + EPS)
    o_ref[...] = (y * w_ref[...]).astype(o_ref.dtype)


def _rmsnorm(x, w, out_dtype, tm=256):
    m, d = x.shape
    return pl.pallas_call(
        _rms_kernel,
        grid=(m // tm,),
        in_specs=[pl.BlockSpec((tm, d), lambda i: (i, 0)),
                  pl.BlockSpec((1, d), lambda i: (0, 0))],
        out_specs=pl.BlockSpec((tm, d), lambda i: (i, 0)),
        out_shape=jax.ShapeDtypeStruct((m, d), out_dtype),
        compiler_params=_params("parallel"),
        name="rmsnorm",
    )(x, w.reshape(1, d))


CAST_ROWS = 512
SHIFT_ROWS = 128


def _mm_kernel(*refs, a_index, n_extra, epilogue, transposed, row_shift):
    n_a, n_w = max(a_index) + 1, len(a_index)
    n_wrefs = n_w * (2 if row_shift else 1)
    a_refs = refs[:n_a]
    w_refs = refs[n_a:n_a + n_wrefs]
    e_refs = refs[n_a + n_wrefs:n_a + n_wrefs + n_extra]
    o_ref = refs[n_a + n_wrefs + n_extra]
    scratch = refs[n_a + n_wrefs + n_extra + 1:]

    @pl.when(pl.program_id(1) == 0)
    def _():
        for p, s in enumerate(scratch):
            rows = s.shape[0]
            if row_shift:
                w, w_next = w_refs[2 * p], w_refs[2 * p + 1]
                for r0 in range(0, rows - row_shift, CAST_ROWS):
                    r1 = min(r0 + CAST_ROWS, rows - row_shift)
                    s[r0:r1, :] = w[r0 + row_shift:r1 + row_shift, :].astype(BF16)
                s[rows - row_shift:rows, :] = w_next[0:row_shift, :].astype(BF16)
            else:
                for r0 in range(0, rows, CAST_ROWS):
                    r1 = min(r0 + CAST_ROWS, rows)
                    s[r0:r1, :] = w_refs[p][r0:r1, :].astype(BF16)

    dims = (_NT if transposed else ((1,), (0,)), ((), ()))
    accs = [lax.dot_general(a_refs[ai][...], s[...], dims, preferred_element_type=F32)
            for ai, s in zip(a_index, scratch)]
    o_ref[...] = epilogue(accs, [e[...] for e in e_refs]).astype(o_ref.dtype)


def _matmul(pairs, extras, epilogue, n, out_dtype, tm, tn, name, transposed=False, row_shift=0):
    m = pairs[0][0].shape[0]
    assert not row_shift or (transposed and row_shift % 16 == 0 and row_shift <= SHIFT_ROWS)
    a_specs, a_args, a_keys, a_index, w_specs, w_args, scratch = [], [], [], [], [], [], []
    for (a, acb, k, w, lead, wkb, wco) in pairs:
        key = (id(a), acb, k)
        if key not in a_keys:
            a_keys.append(key)
            a_args.append(a)
            a_specs.append(pl.BlockSpec((tm, k), lambda j, i, acb=acb: (i, acb)))
        a_index.append(a_keys.index(key))
        squeeze = (None,) * len(lead)
        if transposed:
            w_specs.append(pl.BlockSpec(
                squeeze + (tn, k), lambda j, i, lead=lead, wkb=wkb, wco=wco: (*lead, wco + j, wkb)))
            scratch.append(pltpu.VMEM((tn, k), BF16))
        else:
            w_specs.append(pl.BlockSpec(
                squeeze + (k, tn), lambda j, i, lead=lead, wkb=wkb, wco=wco: (*lead, wkb, wco + j)))
            scratch.append(pltpu.VMEM((k, tn), BF16))
        w_args.append(w)
        if row_shift:
            per = tn // SHIFT_ROWS
            w_specs.append(pl.BlockSpec(
                squeeze + (SHIFT_ROWS, k),
                lambda j, i, lead=lead, wkb=wkb, wco=wco: (*lead, (wco + j + 1) * per, wkb)))
            w_args.append(w)
    e_specs = [pl.BlockSpec((tm, tn), lambda j, i, eco=eco: (i, eco + j)) for (_, eco) in extras]
    kern = functools.partial(_mm_kernel, a_index=tuple(a_index), n_extra=len(extras),
                             epilogue=epilogue, transposed=transposed, row_shift=row_shift)
    return pl.pallas_call(
        kern,
        grid=(n // tn, m // tm),
        in_specs=a_specs + w_specs + e_specs,
        out_specs=pl.BlockSpec((tm, tn), lambda j, i: (i, j)),
        out_shape=jax.ShapeDtypeStruct((m, n), out_dtype),
        scratch_shapes=scratch,
        compiler_params=_params("parallel", "arbitrary"),
        name=name,
    )(*a_args, *w_args, *[e[0] for e in extras])


def _ep_swiglu(accs, ex):
    return _silu(accs[0]) * accs[1]


def _ep_identity(accs, ex):
    return accs[0]


def _ep_residual(scale, accs, ex):
    return ex[0] + scale * accs[0]


def _ep_ple(accs, ex):
    return ex[0] + _sigmoid(accs[0]) * accs[1]


def _ep_merge(accs, ex):
    out = _sigmoid(ex[0]) * accs[0]
    for b in range(1, N_BRANCH):
        out = out + _sigmoid(ex[b]) * accs[b]
    return out


def _rope(x, cos, sin_signed):
    return x * cos + pltpu.roll(x, x.shape[-1] // 2, axis=x.ndim - 1) * sin_signed


def _retention_kernel(q_ref, k_ref, v_ref, g_ref, cos_ref, sin_ref, lg_ref, o_ref,
                      kr_ref, st_ref, *, seq):
    c = RET_CHUNK
    n = seq // c
    lg = lg_ref[0:1, :]
    lgk = lg[:, :RET_DK]
    ri = lax.broadcasted_iota(jnp.int32, (c, RET_DK), 0).astype(F32)
    rj = lax.broadcasted_iota(jnp.int32, (c, c), 1).astype(F32)
    q_dec_f = jnp.exp(lgk * (ri + 1.0))
    k_dec_f = jnp.exp(lgk * (c - 1.0 - ri))
    q_dec_b = jnp.exp(lgk * (c - ri))
    k_dec_b = jnp.exp(lgk * ri)
    intra_decay = jnp.exp(lgk * jnp.abs(ri[:, :c] - rj))
    chunk_decay = jnp.exp(lg * float(c))
    scale = RET_DK ** -0.5

    def rows(i):
        return pl.ds(pl.multiple_of(i * c, c), c)

    def fwd(i, state):
        r = rows(i)
        kr = _rope(k_ref[r, :], cos_ref[r, :], sin_ref[r, :])
        kr_ref[r, :] = kr
        st_ref[i] = state
        return chunk_decay * state + _bdot_t(kr * k_dec_f, v_ref[r, :], _TN)

    lax.fori_loop(0, n, fwd, jnp.zeros((RET_DK, RET_DV), F32))

    def bwd(t, state):
        i = n - 1 - t
        r = rows(i)
        q = _rope(q_ref[r, :], cos_ref[r, :], sin_ref[r, :]) * scale
        kr = kr_ref[r, :]
        v = v_ref[r, :]
        scores = _bdot_t(q, kr, _NT) * intra_decay
        o = _bdot(scores, v) + _bdot(q * q_dec_f, st_ref[i]) + _bdot(q * q_dec_b, state)
        o = o * lax.rsqrt(jnp.mean(o * o, axis=-1, keepdims=True) + EPS)
        o_ref[r, :] = (o * _silu(g_ref[r, :])).astype(o_ref.dtype)
        return chunk_decay * state + _bdot_t(kr * k_dec_b, v, _TN)

    lax.fori_loop(0, n, bwd, jnp.zeros((RET_DK, RET_DV), F32))


def _retention(p1, cos, sin_signed, batch, seq):
    lg = jnp.log(1.0 - 2.0 ** (-5.0 - jnp.arange(RET_HEADS, dtype=F32)))
    lg = jnp.broadcast_to(lg[:, None, None], (RET_HEADS, 8, RET_DV))
    kq, kv = RET_HEADS, (2 * RET_HEADS * RET_DK) // RET_DV
    return pl.pallas_call(
        functools.partial(_retention_kernel, seq=seq),
        grid=(batch, RET_HEADS),
        in_specs=[
            pl.BlockSpec((seq, RET_DK), lambda b, h: (b, h)),
            pl.BlockSpec((seq, RET_DK), lambda b, h: (b, kq + h)),
            pl.BlockSpec((seq, RET_DV), lambda b, h: (b, kv + h)),
            pl.BlockSpec((seq, RET_DV), lambda b, h: (b, kv + RET_HEADS + h)),
            pl.BlockSpec((seq, RET_DK), lambda b, h: (b, 0)),
            pl.BlockSpec((seq, RET_DK), lambda b, h: (b, 0)),
            pl.BlockSpec((None, 8, RET_DV), lambda b, h: (h, 0, 0)),
        ],
        out_specs=pl.BlockSpec((seq, RET_DV), lambda b, h: (b, h)),
        out_shape=jax.ShapeDtypeStruct((batch * seq, RET_HEADS * RET_DV), BF16),
        scratch_shapes=[pltpu.VMEM((seq, RET_DK), F32),
                        pltpu.VMEM((seq // RET_CHUNK, RET_DK, RET_DV), F32)],
        compiler_params=_params("parallel", "parallel"),
        name="retention",
    )(p1, p1, p1, p1, cos, sin_signed, lg)


def _split2(x):
    hi = x.astype(BF16)
    lo = (x - hi.astype(F32)).astype(BF16)
    return hi, lo


def _wide_dot_rhs01(x, sel):
    hi, lo = _split2(x)
    return jnp.dot(jnp.concatenate([hi, lo], axis=1), jnp.concatenate([sel, sel], axis=0),
                   preferred_element_type=F32)


def _wide_dot_lhs01(sel, x):
    hi, lo = _split2(x)
    r = jnp.dot(sel, jnp.concatenate([hi, lo], axis=1), preferred_element_type=F32)
    return r[:, :LANES] + r[:, LANES:]


def _softplus(x):
    return jnp.maximum(x, 0.0) + jnp.log1p(jnp.exp(-jnp.abs(x)))


def _conv_silu_block(x_ref, w_ref, r0, rows, seq):
    mid = x_ref[pl.ds(r0, rows), :]
    top = x_ref[pl.ds(pl.multiple_of(jnp.maximum(r0 - 8, 0), 8), 8), :]
    top = jnp.where(r0 > 0, top, 0.0)
    bot = x_ref[pl.ds(pl.multiple_of(jnp.minimum(r0 + rows, seq - 8), 8), 8), :]
    bot = jnp.where(r0 + rows < seq, bot, 0.0)
    win = jnp.concatenate([top, mid, bot], axis=0)
    half = CONV_WIDTH // 2
    acc = None
    for t in range(CONV_WIDTH):
        sh = win if t == half else pltpu.roll(win, (half - t) % (rows + 16), axis=0)
        term = sh[8:8 + rows, :] * w_ref[t:t + 1, :]
        acc = term if acc is None else acc + term
    return _silu(acc)


def _l2norm(x):
    return x * lax.rsqrt(jnp.sum(x * x, axis=-1, keepdims=True) + EPS)


def _gdn_prepare(chunks):
    c = GDN_CHUNK
    ri = lax.broadcasted_iota(jnp.int32, (c, c), 0)
    ci = lax.broadcasted_iota(jnp.int32, (c, c), 1)
    eye = ri == ci
    incl = {False: ri >= ci, True: ri <= ci}
    strict = {False: ri > ci, True: ri < ci}
    incl_bf = {up: m.astype(BF16) for up, m in incl.items()}
    ups = [ch[5] for ch in chunks]
    gcs = [_wide_dot_lhs01(incl_bf[up], ch[3]) for ch, up in zip(chunks, ups)]
    tots = [gc[0:1, :] if up else gc[c - 1:c, :] for gc, up in zip(gcs, ups)]
    decays = []
    for gc, up in zip(gcs, ups):
        gci = jnp.concatenate([gc, gc], axis=1)
        gcj = jnp.sum(jnp.where(eye, gci, 0.0), axis=0, keepdims=True)
        decays.append(jnp.where(incl[up], jnp.exp(jnp.where(incl[up], gci - gcj, 0.0)), 0.0))
    egcs = [jnp.exp(gc) for gc in gcs]
    kbs = [ch[1] * ch[4] for ch in chunks]
    a_mats = [jnp.where(strict[up], _bdot_t(kb, ch[1], _NT) * dec, 0.0)
              for kb, ch, dec, up in zip(kbs, chunks, decays, ups)]
    qks = [_bdot_t(ch[0], ch[1], _NT) * dec for ch, dec in zip(chunks, decays)]
    invs = _unit_triangular_inverse(a_mats, ri, ci, eye)
    sols = [_bdot(inv, jnp.concatenate([ch[2] * ch[4], kb * egc], axis=1))
            for inv, ch, kb, egc in zip(invs, chunks, kbs, egcs)]
    return [(sol[:, :GDN_DV], sol[:, GDN_DV:], qk, ch[0] * egc, ch[1] * jnp.exp(tot - gc), jnp.exp(tot))
            for sol, qk, ch, egc, tot, gc in zip(sols, qks, chunks, egcs, tots, gcs)]


def _unit_triangular_inverse(a_mats, ri, ci, eye):
    c = a_mats[0].shape[0]
    base = 16
    same = (ri // base) == (ci // base)
    pws = [jnp.where(same, -a, 0.0) for a in a_mats]
    invs = [jnp.where(eye, 1.0, 0.0) + x for x in pws]
    for _ in range(int(math.log2(base)) - 1):
        pws = [_bdot(pw, pw) for pw in pws]
        invs = [inv + _bdot(inv, pw) for inv, pw in zip(invs, pws)]
    size = base
    while size < c:
        wider = (ri // (2 * size)) == (ci // (2 * size))
        pick = wider & jnp.logical_not(same)
        inner = [_bdot(jnp.where(pick, a, 0.0), inv) for a, inv in zip(a_mats, invs)]
        invs = [inv - _bdot(inv, e) for inv, e in zip(invs, inner)]
        same, size = wider, 2 * size
    return invs


def _gdn_chunk_step(pre, state):
    u, w, qk, q_in, k_out, e_tot = pre
    v_new = u - _bdot(w, state)
    o = _bdot(q_in, state) + _bdot(qk, v_new)
    state = state * e_tot + _bdot_t(k_out, v_new, _TN)
    return o, state


def _gdn_kernel(q_ref, k_ref, v_ref, z_ref, ab_ref, wq_ref, wk_ref, wv_ref,
                alog_ref, dt_ref, nw_ref, o_ref, qn_ref, kn_ref, vn_ref, acc_ref, *, seq):
    c = GDN_CHUNK
    n = seq // c
    h = pl.program_id(1)

    def prep(i, carry):
        r0 = pl.multiple_of(i * c, c)
        r = pl.ds(r0, c)
        qn_ref[r, :] = _l2norm(_conv_silu_block(q_ref, wq_ref, r0, c, seq)) * (GDN_DK ** -0.5)
        kn_ref[r, :] = _l2norm(_conv_silu_block(k_ref, wk_ref, r0, c, seq))
        vn_ref[r, :] = _conv_silu_block(v_ref, wv_ref, r0, c, seq)
        acc_ref[r, :] = jnp.zeros((c, GDN_DV), F32)
        return carry

    lax.fori_loop(0, n, prep, 0)

    lane = lax.broadcasted_iota(jnp.int32, (1, LANES), 1)
    sel_row = lax.broadcasted_iota(jnp.int32, (LANES, 2 * LANES), 0)
    sel_col = lax.broadcasted_iota(jnp.int32, (LANES, 2 * LANES), 1)

    def gates(r, d):
        x = ab_ref[r, :]
        gall = -jnp.exp(alog_ref[...]) * _softplus(x + dt_ref[...])
        y = jnp.where(lane < 2 * GDN_HEADS, gall, _sigmoid(x))
        g_lane = d * GDN_HEADS + h
        want = jnp.where(sel_col < LANES, g_lane, g_lane + 2 * GDN_HEADS)
        both = _wide_dot_rhs01(y, (sel_row == want).astype(BF16))
        return both[:, :LANES], both[:, LANES:]

    def step(i, states):
        sf, sb = states
        which = []
        for t in range(GDN_UNROLL):
            which += [(i * GDN_UNROLL + t, 0), (n - 1 - i * GDN_UNROLL - t, 1)]
        rows = [pl.ds(pl.multiple_of(j * c, c), c) for j, _ in which]
        gb = [gates(r, d) for r, (_, d) in zip(rows, which)]
        pre = _gdn_prepare([(qn_ref[r, :], kn_ref[r, :], vn_ref[r, :], g, beta, d == 1)
                            for r, (g, beta), (_, d) in zip(rows, gb, which)])
        outs = []
        for t in range(GDN_UNROLL):
            of, sf = _gdn_chunk_step(pre[2 * t], sf)
            ob, sb = _gdn_chunk_step(pre[2 * t + 1], sb)
            outs += [of, ob]
        for r, o in zip(rows, outs):
            acc_ref[r, :] += o
        return sf, sb

    zero = jnp.zeros((GDN_DK, GDN_DV), F32)
    lax.fori_loop(0, n // GDN_UNROLL, step, (zero, zero))

    def finish(i, carry):
        r = pl.ds(pl.multiple_of(i * c, c), c)
        o = acc_ref[r, :]
        o = o * lax.rsqrt(jnp.mean(o * o, axis=-1, keepdims=True) + EPS) * nw_ref[...]
        o_ref[r, :] = (o * _silu(z_ref[r, :])).astype(o_ref.dtype)
        return carry

    lax.fori_loop(0, n, finish, 0)


def _gdn(p1, pab, conv_w, a_log, dt_bias, norm_w, qkv_off, z_off, batch, seq):
    hh = GDN_HEADS
    pad = lambda t: jnp.pad(t.reshape(1, 2 * hh).astype(F32), ((0, 0), (0, LANES - 2 * hh)))
    tok = lambda off: pl.BlockSpec((seq, LANES), lambda b, h, off=off: (b, off + h))
    cw = lambda off: pl.BlockSpec((CONV_WIDTH, LANES), lambda b, h, off=off: (0, off + h))
    row = pl.BlockSpec((1, LANES), lambda b, h: (0, 0))
    return pl.pallas_call(
        functools.partial(_gdn_kernel, seq=seq),
        grid=(batch, hh),
        in_specs=[tok(qkv_off), tok(qkv_off + hh), tok(qkv_off + 2 * hh), tok(z_off),
                  pl.BlockSpec((seq, LANES), lambda b, h: (b, 0)),
                  cw(0), cw(hh), cw(2 * hh), row, row, row],
        out_specs=pl.BlockSpec((seq, GDN_DV), lambda b, h: (b, h)),
        out_shape=jax.ShapeDtypeStruct((batch * seq, hh * GDN_DV), BF16),
        scratch_shapes=[pltpu.VMEM((seq, LANES), F32)] * 4,
        compiler_params=_params("parallel", "parallel"),
        name="gdn",
    )(p1, p1, p1, p1, pab, conv_w, conv_w, conv_w, pad(a_log), pad(dt_bias),
      norm_w.reshape(1, GDN_DV).astype(F32))


def _diff_kernel(q_ref, k_ref, v_ref, cq_ref, sq_ref, ck_ref, sk_ref, lam_ref, nw_ref, o_ref,
                 kr_ref, vb_ref, *, lambda_init):
    d = DIFF_D

    @pl.when(pl.program_id(2) == 0)
    def _():
        for t in range(2):
            kr_ref[t] = _rope(k_ref[:, t * d:(t + 1) * d], ck_ref[...], sk_ref[...]).astype(BF16)
        vb_ref[...] = v_ref[...].astype(BF16)

    lp = lam_ref[...]
    lam = (jnp.exp(jnp.sum(lp[0:1] * lp[1:2], axis=-1, keepdims=True))
           - jnp.exp(jnp.sum(lp[2:3] * lp[3:4], axis=-1, keepdims=True)) + lambda_init)
    tq = q_ref.shape[0]
    halves = [slice(0, tq // 2), slice(tq // 2, tq)]
    scores = []
    for t in range(2):
        q = _rope(q_ref[:, t * d:(t + 1) * d], cq_ref[...], sq_ref[...]) * (d ** -0.5 * LOG2_E)
        q = q.astype(BF16)
        for r in halves:
            scores.append(lax.dot_general(q[r], kr_ref[t], (_NT, ((), ())),
                                          preferred_element_type=F32))
    outs = []
    for s in scores:
        p = jnp.exp2(s - jnp.max(s, axis=-1, keepdims=True))
        l = jnp.sum(p, axis=-1, keepdims=True)
        outs.append(jnp.dot(p.astype(BF16), vb_ref[...], preferred_element_type=F32) / l)
    outs = [jnp.concatenate(outs[0:2], axis=0), jnp.concatenate(outs[2:4], axis=0)]
    o = outs[0] - lam * outs[1]
    o = o * lax.rsqrt(jnp.mean(o * o, axis=-1, keepdims=True) + EPS) * nw_ref[...]
    o_ref[...] = (o * (1.0 - lambda_init)).astype(o_ref.dtype)


def _diff_attention(p2, cos, sin_signed, lam_params, subln_w, lambda_init, q_off, batch, seq, tq):
    nq = seq // tq
    hh = DIFF_HEADS
    return pl.pallas_call(
        functools.partial(_diff_kernel, lambda_init=lambda_init),
        grid=(batch, hh, nq),
        in_specs=[
            pl.BlockSpec((tq, 2 * DIFF_D), lambda b, h, i: (b * nq + i, q_off + h)),
            pl.BlockSpec((seq, 2 * DIFF_D), lambda b, h, i: (b, q_off + hh + h)),
            pl.BlockSpec((seq, DIFF_DV), lambda b, h, i: (b, q_off + 2 * hh + h)),
            pl.BlockSpec((tq, DIFF_D), lambda b, h, i: (b * nq + i, 0)),
            pl.BlockSpec((tq, DIFF_D), lambda b, h, i: (b * nq + i, 0)),
            pl.BlockSpec((seq, DIFF_D), lambda b, h, i: (b, 0)),
            pl.BlockSpec((seq, DIFF_D), lambda b, h, i: (b, 0)),
            pl.BlockSpec((4, DIFF_D), lambda b, h, i: (0, 0)),
            pl.BlockSpec((1, DIFF_DV), lambda b, h, i: (0, 0)),
        ],
        out_specs=pl.BlockSpec((tq, DIFF_DV), lambda b, h, i: (b * nq + i, h)),
        out_shape=jax.ShapeDtypeStruct((batch * seq, hh * DIFF_DV), BF16),
        scratch_shapes=[pltpu.VMEM((2, seq, DIFF_D), BF16), pltpu.VMEM((seq, DIFF_DV), BF16)],
        compiler_params=_params("parallel", "parallel", "arbitrary"),
        name="diff_attention",
    )(p2, p2, p2, cos, sin_signed, cos, sin_signed, lam_params.astype(F32),
      subln_w.reshape(1, DIFF_DV).astype(F32))


def _rope_tables(positions, dim):
    inv = 1.0 / (ROPE_THETA ** (jnp.arange(0, dim, 2, dtype=F32) / dim))
    ang = positions.astype(F32)[..., None] * inv
    ang = jnp.concatenate([ang, ang], axis=-1).reshape(-1, dim)
    sign = jnp.concatenate([-jnp.ones((dim // 2,), F32), jnp.ones((dim // 2,), F32)])
    return jnp.cos(ang), jnp.sin(ang) * sign


def _ffn(x, ln, w_gate, w_up, w_down, lead, d_model, d_ff):
    h = _rmsnorm(x, ln, BF16)
    act = _matmul([(h, 0, d_model, w_gate, lead, 0, 0), (h, 0, d_model, w_up, lead, 0, 0)], [],
                  _ep_swiglu, d_ff, BF16, tm=512, tn=512, name="ffn_up")
    half = d_ff // 2
    for kb in range(2):
        x = _matmul([(act, kb, half, w_down, lead, kb, 0)], [(x, 0)],
                    functools.partial(_ep_residual, 0.5), d_model, F32,
                    tm=1024, tn=512, name="ffn_down")
    return x


def kernel(x, p, positions, ln_ffn, ffn_w_gate, ffn_w_up, ffn_w_down, ln_mix, w_in, conv_w,
           gdn_a_log, gdn_dt_bias, gdn_norm_w, diff_lambda, diff_subln_w, w_branch, w_out,
           ln_ple, w_ple_gate, w_ple_proj, final_norm):
    batch, seq, d_model = x.shape
    depth = p.shape[0]
    ple_dim = p.shape[-1]
    d_ff = ffn_w_gate.shape[-1]
    t = batch * seq
    cos, sin_signed = _rope_tables(positions, RET_DK)

    n_ret = RET_HEADS * (2 * RET_DK + 2 * RET_DV)
    n_gdn = GDN_HEADS * (2 * GDN_DK + 2 * GDN_DV)
    n1 = n_ret + n_gdn
    n_ab = 4 * GDN_HEADS
    n_diff = DIFF_HEADS * (4 * DIFF_D + DIFF_DV)
    n2 = n_diff + N_BRANCH * d_model
    tn_in = 512

    x = x.reshape(t, d_model)
    p = p.reshape(depth, t, ple_dim)
    w_in_t = jnp.swapaxes(w_in, 1, 2)
    for i in range(depth):
        lambda_init = 0.8 - 0.6 * math.exp(-0.3 * i)
        x = _ffn(x, ln_ffn[i, 0], ffn_w_gate, ffn_w_up, ffn_w_down, (i, 0), d_model, d_ff)

        h = _rmsnorm(x, ln_mix[i], BF16)
        p1 = _matmul([(h, 0, d_model, w_in_t, (i,), 0, 0)], [], _ep_identity, n1, F32,
                     tm=512, tn=2 * tn_in, name="w_in_head", transposed=True)
        pab = _matmul([(h, 0, d_model, w_in_t, (i,), 0, n1 // LANES)], [], _ep_identity, LANES, F32,
                      tm=1024, tn=LANES, name="w_in_gates", transposed=True)
        p2 = _matmul([(h, 0, d_model, w_in_t, (i,), 0, n1 // tn_in)], [], _ep_identity, n2, F32,
                     tm=1024, tn=tn_in, name="w_in_tail", transposed=True, row_shift=n_ab)

        y_ret = _retention(p1, cos, sin_signed, batch, seq)
        y_gdn = _gdn(p1, pab, conv_w[i], gdn_a_log[i], gdn_dt_bias[i], gdn_norm_w[i],
                     n_ret // LANES, (n_ret + GDN_HEADS * (2 * GDN_DK + GDN_DV)) // LANES,
                     batch, seq)
        y_diff = _diff_attention(p2, cos, sin_signed, diff_lambda[i], diff_subln_w[i],
                                 lambda_init, 0, batch, seq, tq=256)

        bw = RET_HEADS * RET_DV
        tn_m = 512
        gate_off = n_diff // tn_m
        merged = _matmul(
            [(y, 0, bw, w_branch, (i, b), 0, 0) for b, y in enumerate((y_ret, y_gdn, y_diff))],
            [(p2, gate_off + b * (d_model // tn_m)) for b in range(N_BRANCH)],
            _ep_merge, d_model, BF16, tm=512, tn=tn_m, name="merge")
        x = _matmul([(merged, 0, d_model, w_out, (i,), 0, 0)], [(x, 0)],
                    functools.partial(_ep_residual, 1.0), d_model, F32,
                    tm=1024, tn=512, name="w_out")

        x = _ffn(x, ln_ffn[i, 1], ffn_w_gate, ffn_w_up, ffn_w_down, (i, 1), d_model, d_ff)

        h = _rmsnorm(x, ln_ple[i], BF16)
        x = _matmul([(h, 0, d_model, w_ple_gate, (i,), 0, 0),
                     (p[i].astype(BF16), 0, ple_dim, w_ple_proj, (i,), 0, 0)], [(x, 0)],
                    _ep_ple, d_model, F32, tm=1024, tn=512, name="ple")
    return _rmsnorm(x, final_norm, F32).reshape(batch, seq, d_model)
```

```python
import functools
import math

import jax
import jax.numpy as jnp
from jax import lax
from jax.experimental import pallas as pl
from jax.experimental.pallas import tpu as pltpu

F32 = jnp.float32
BF16 = jnp.bfloat16

EPS = 1e-6
ROPE_THETA = 10000.0
LOG2_E = math.log2(math.e)
LANES = 128
VMEM_LIMIT = 62 * 1024 * 1024

RET_HEADS, RET_DK, RET_DV, RET_CHUNK = 8, 128, 256, 128
RET_UNROLL = 4
GDN_HEADS, GDN_DK, GDN_DV, CONV_WIDTH = 16, 128, 128, 5
GDN_CHUNK = 256
GDN_UNROLL = 4
DIFF_HEADS, DIFF_D, DIFF_DV = 8, 128, 256
DIFF_ROWS = 256
N_BRANCH = 3


def _params(*sem):
    return pltpu.CompilerParams(dimension_semantics=sem, vmem_limit_bytes=VMEM_LIMIT)


def _bdot(a, b):
    return jnp.dot(a.astype(BF16), b.astype(BF16), preferred_element_type=F32)


def _bdot_t(a, b, dims):
    return lax.dot_general(a.astype(BF16), b.astype(BF16), (dims, ((), ())),
                           preferred_element_type=F32)


_NT = ((1,), (1,))
_TN = ((0,), (0,))


def _fdot(a, b):
    return jnp.dot(a, b, preferred_element_type=F32, precision=lax.Precision.HIGHEST)


def _sigmoid(x):
    return 1.0 / (1.0 + jnp.exp(-x))


def _silu(x):
    return x * _sigmoid(x)


def _rms_kernel(x_ref, w_ref, o_ref):
    x = x_ref[...]
    y = x * lax.rsqrt(jnp.mean(x * x, axis=-1, keepdims=True) + EPS)
    o_ref[...] = (y * w_ref[...]).astype(o_ref.dtype)


def _rmsnorm(x, w, out_dtype, tm=256):
    m, d = x.shape
    return pl.pallas_call(
        _rms_kernel,
        grid=(m // tm,),
        in_specs=[pl.BlockSpec((tm, d), lambda i: (i, 0)),
                  pl.BlockSpec((1, d), lambda i: (0, 0))],
        out_specs=pl.BlockSpec((tm, d), lambda i: (i, 0)),
        out_shape=jax.ShapeDtypeStruct((m, d), out_dtype),
        compiler_params=_params("parallel"),
        name="rmsnorm",
    )(x, w.reshape(1, d))


def _mm_kernel(*refs, a_index, n_extra, epilogue, transposed, split):
    n_a, n_w = max(a_index) + 1, len(a_index)
    a_refs = refs[:n_a]
    w_refs = refs[n_a:n_a + n_w * split]
    e_refs = refs[n_a + n_w * split:n_a + n_w * split + n_extra]
    o_ref = refs[n_a + n_w * split + n_extra]
    scratch = refs[n_a + n_w * split + n_extra + 1:]
    jj, i = pl.program_id(0), pl.program_id(1)

    slot = jj % 2
    for p, s in enumerate(scratch):
        for c in range(split):
            w = w_refs[p * split + c]
            ch = w.shape[0]
            r0 = pl.multiple_of((i * split + c) * ch, ch)
            s[slot, pl.ds(r0, ch), :] = w[...].astype(BF16)

    @pl.when(jj >= 1)
    def _():
        dims = (_NT if transposed else ((1,), (0,)), ((), ()))
        accs = [lax.dot_general(a_refs[ai][...], s[1 - slot], dims, preferred_element_type=F32)
                for ai, s in zip(a_index, scratch)]
        o_ref[...] = epilogue(accs, [e[...] for e in e_refs]).astype(o_ref.dtype)


def _matmul(pairs, extras, epilogue, n, out_dtype, tm, tn, name, transposed=False, row_shift=0):
    m = pairs[0][0].shape[0]
    n_i, n_j = m // tm, n // tn
    split = tn // (n_i * row_shift) if row_shift else 1
    row = lambda jj, i: jnp.where(jj >= 1, i, 0)
    col = lambda jj: jnp.maximum(jj - 1, 0)
    nxt = lambda jj: jnp.minimum(jj, n_j - 1)
    a_specs, a_args, a_keys, a_index, w_specs, scratch = [], [], [], [], [], []
    for (a, acb, k, w, lead, wkb, wco) in pairs:
        key = (id(a), acb, k)
        if key not in a_keys:
            a_keys.append(key)
            a_args.append(a)
            a_specs.append(pl.BlockSpec((tm, k), lambda jj, i, acb=acb: (row(jj, i), acb)))
        a_index.append(a_keys.index(key))
        squeeze = (None,) * len(lead)
        if transposed:
            ch = tn // (n_i * split)
            assert ch % 16 == 0 and row_shift % ch == 0
            for c in range(split):
                w_specs.append(pl.BlockSpec(
                    squeeze + (ch, k),
                    lambda jj, i, lead=lead, wkb=wkb, wco=wco, c=c:
                        (*lead, ((wco + nxt(jj)) * n_i + i) * split + c + row_shift // ch, wkb)))
            scratch.append(pltpu.VMEM((2, tn, k), BF16))
        else:
            ch = k // (n_i * split)
            assert ch % 16 == 0 and not row_shift
            for c in range(split):
                w_specs.append(pl.BlockSpec(
                    squeeze + (ch, tn),
                    lambda jj, i, lead=lead, wkb=wkb, wco=wco, c=c:
                        (*lead, (wkb * n_i + i) * split + c, wco + nxt(jj))))
            scratch.append(pltpu.VMEM((2, k, tn), BF16))
    e_specs = [pl.BlockSpec((tm, tn), lambda jj, i, eco=eco: (row(jj, i), eco + col(jj)))
               for (_, eco) in extras]
    kern = functools.partial(_mm_kernel, a_index=tuple(a_index), n_extra=len(extras),
                             epilogue=epilogue, transposed=transposed, split=split)
    return pl.pallas_call(
        kern,
        grid=(n_j + 1, n_i),
        in_specs=a_specs + w_specs + e_specs,
        out_specs=pl.BlockSpec((tm, tn), lambda jj, i: (row(jj, i), col(jj))),
        out_shape=jax.ShapeDtypeStruct((m, n), out_dtype),
        scratch_shapes=scratch,
        compiler_params=_params("arbitrary", "arbitrary"),
        name=name,
    )(*a_args, *[p[3] for p in pairs for _ in range(split)], *[e[0] for e in extras])


def _ep_swiglu(accs, ex):
    return _silu(accs[0]) * accs[1]


def _ep_identity(accs, ex):
    return accs[0]


def _ep_residual(scale, accs, ex):
    return ex[0] + scale * accs[0]


def _ep_ple(accs, ex):
    return ex[0] + _sigmoid(accs[0]) * accs[1]


def _ep_merge(accs, ex):
    out = _sigmoid(ex[0]) * accs[0]
    for b in range(1, N_BRANCH):
        out = out + _sigmoid(ex[b]) * accs[b]
    return out


def _rope(x, cos, sin_signed):
    return x * cos + pltpu.roll(x, x.shape[-1] // 2, axis=x.ndim - 1) * sin_signed


def _retention_kernel(q_ref, k_ref, v_ref, g_ref, cos_ref, sin_ref, lg_ref, o_ref,
                      kr_ref, st_ref, *, seq):
    c = RET_CHUNK
    n = seq // c
    lg = lg_ref[0:1, :]
    lgk = lg[:, :RET_DK]
    ri = lax.broadcasted_iota(jnp.int32, (c, RET_DK), 0).astype(F32)
    rj = lax.broadcasted_iota(jnp.int32, (c, c), 1).astype(F32)
    q_dec_f = jnp.exp(lgk * (ri + 1.0))
    k_dec_f = jnp.exp(lgk * (c - 1.0 - ri))
    q_dec_b = jnp.exp(lgk * (c - ri))
    k_dec_b = jnp.exp(lgk * ri)
    intra_decay = jnp.exp(lgk * jnp.abs(ri[:, :c] - rj))
    chunk_decay = jnp.exp(lg * float(c))
    scale = RET_DK ** -0.5

    def rows(i):
        return pl.ds(pl.multiple_of(i * c, c), c)

    u_n = RET_UNROLL

    def fwd(it, state):
        idx = [it * u_n + u for u in range(u_n)]
        rs = [rows(i) for i in idx]
        krs = [_rope(k_ref[r, :], cos_ref[r, :], sin_ref[r, :]) for r in rs]
        for r, kr in zip(rs, krs):
            kr_ref[r, :] = kr
        kvs = [_bdot_t(kr * k_dec_f, v_ref[r, :], _TN) for r, kr in zip(rs, krs)]
        for i, kv in zip(idx, kvs):
            st_ref[i] = state
            state = chunk_decay * state + kv
        return state

    lax.fori_loop(0, n // u_n, fwd, jnp.zeros((RET_DK, RET_DV), F32))

    def bwd(it, state):
        idx = [n - 1 - (it * u_n + u) for u in range(u_n)]
        rs = [rows(i) for i in idx]
        qs = [_rope(q_ref[r, :], cos_ref[r, :], sin_ref[r, :]) * scale for r in rs]
        krs = [kr_ref[r, :] for r in rs]
        vs = [v_ref[r, :] for r in rs]
        scores = [_bdot_t(q, kr, _NT) * intra_decay for q, kr in zip(qs, krs)]
        local = [_bdot(s, v) + _bdot(q * q_dec_f, st_ref[i])
                 for s, v, q, i in zip(scores, vs, qs, idx)]
        kvs = [_bdot_t(kr * k_dec_b, v, _TN) for kr, v in zip(krs, vs)]
        outs = []
        for o, q, kv in zip(local, qs, kvs):
            outs.append(o + _bdot(q * q_dec_b, state))
            state = chunk_decay * state + kv
        for r, o in zip(rs, outs):
            o = o * lax.rsqrt(jnp.mean(o * o, axis=-1, keepdims=True) + EPS)
            o_ref[r, :] = (o * _silu(g_ref[r, :])).astype(o_ref.dtype)
        return state

    lax.fori_loop(0, n // u_n, bwd, jnp.zeros((RET_DK, RET_DV), F32))


def _retention(p1, cos, sin_signed, batch, seq):
    lg = jnp.log(1.0 - 2.0 ** (-5.0 - jnp.arange(RET_HEADS, dtype=F32)))
    lg = jnp.broadcast_to(lg[:, None, None], (RET_HEADS, 8, RET_DV))
    kq, kv = RET_HEADS, (2 * RET_HEADS * RET_DK) // RET_DV
    return pl.pallas_call(
        functools.partial(_retention_kernel, seq=seq),
        grid=(batch, RET_HEADS),
        in_specs=[
            pl.BlockSpec((seq, RET_DK), lambda b, h: (b, h)),
            pl.BlockSpec((seq, RET_DK), lambda b, h: (b, kq + h)),
            pl.BlockSpec((seq, RET_DV), lambda b, h: (b, kv + h)),
            pl.BlockSpec((seq, RET_DV), lambda b, h: (b, kv + RET_HEADS + h)),
            pl.BlockSpec((seq, RET_DK), lambda b, h: (b, 0)),
            pl.BlockSpec((seq, RET_DK), lambda b, h: (b, 0)),
            pl.BlockSpec((None, 8, RET_DV), lambda b, h: (h, 0, 0)),
        ],
        out_specs=pl.BlockSpec((seq, RET_DV), lambda b, h: (b, h)),
        out_shape=jax.ShapeDtypeStruct((batch * seq, RET_HEADS * RET_DV), BF16),
        scratch_shapes=[pltpu.VMEM((seq, RET_DK), F32),
                        pltpu.VMEM((seq // RET_CHUNK, RET_DK, RET_DV), F32)],
        compiler_params=_params("parallel", "parallel"),
        name="retention",
    )(p1, p1, p1, p1, cos, sin_signed, lg)


def _split2(x):
    hi = x.astype(BF16)
    lo = (x - hi.astype(F32)).astype(BF16)
    return hi, lo


def _wide_dot_rhs01(x, sel):
    hi, lo = _split2(x)
    return jnp.dot(jnp.concatenate([hi, lo], axis=1), jnp.concatenate([sel, sel], axis=0),
                   preferred_element_type=F32)


def _wide_dot_lhs01(sel, x):
    hi, lo = _split2(x)
    r = jnp.dot(sel, jnp.concatenate([hi, lo], axis=1), preferred_element_type=F32)
    return r[:, :LANES] + r[:, LANES:]


def _softplus(x):
    return jnp.maximum(x, 0.0) + jnp.log1p(jnp.exp(-jnp.abs(x)))


def _conv_silu_block(x_ref, w_ref, r0, rows, seq):
    mid = x_ref[pl.ds(r0, rows), :]
    top = x_ref[pl.ds(pl.multiple_of(jnp.maximum(r0 - 8, 0), 8), 8), :]
    top = jnp.where(r0 > 0, top, 0.0)
    bot = x_ref[pl.ds(pl.multiple_of(jnp.minimum(r0 + rows, seq - 8), 8), 8), :]
    bot = jnp.where(r0 + rows < seq, bot, 0.0)
    win = jnp.concatenate([top, mid, bot], axis=0)
    half = CONV_WIDTH // 2
    acc = None
    for t in range(CONV_WIDTH):
        sh = win if t == half else pltpu.roll(win, (half - t) % (rows + 16), axis=0)
        term = sh[8:8 + rows, :] * w_ref[t:t + 1, :]
        acc = term if acc is None else acc + term
    return _silu(acc)


def _l2norm(x):
    return x * lax.rsqrt(jnp.sum(x * x, axis=-1, keepdims=True) + EPS)


def _gdn_prepare(chunks):
    c = GDN_CHUNK
    ri = lax.broadcasted_iota(jnp.int32, (c, c), 0)
    ci = lax.broadcasted_iota(jnp.int32, (c, c), 1)
    eye = ri == ci
    incl = {False: ri >= ci, True: ri <= ci}
    strict = {False: ri > ci, True: ri < ci}
    incl_bf = {up: m.astype(BF16) for up, m in incl.items()}
    ups = [ch[5] for ch in chunks]
    gcs = [_wide_dot_lhs01(incl_bf[up], ch[3]) for ch, up in zip(chunks, ups)]
    tots = [gc[0:1, :] if up else gc[c - 1:c, :] for gc, up in zip(gcs, ups)]
    decays = []
    for gc, up in zip(gcs, ups):
        gci = jnp.concatenate([gc, gc], axis=1)
        gcj = jnp.sum(jnp.where(eye, gci, 0.0), axis=0, keepdims=True)
        decays.append(jnp.where(incl[up], jnp.exp(jnp.where(incl[up], gci - gcj, 0.0)), 0.0))
    egcs = [jnp.exp(gc) for gc in gcs]
    kbs = [ch[1] * ch[4] for ch in chunks]
    a_mats = [jnp.where(strict[up], _bdot_t(kb, ch[1], _NT) * dec, 0.0)
              for kb, ch, dec, up in zip(kbs, chunks, decays, ups)]
    qks = [_bdot_t(ch[0], ch[1], _NT) * dec for ch, dec in zip(chunks, decays)]
    invs = _unit_triangular_inverse(a_mats, ri, ci, eye)
    sols = [_bdot(inv, jnp.concatenate([ch[2] * ch[4], kb * egc], axis=1))
            for inv, ch, kb, egc in zip(invs, chunks, kbs, egcs)]
    return [(sol[:, :GDN_DV], sol[:, GDN_DV:], qk, ch[0] * egc, ch[1] * jnp.exp(tot - gc), jnp.exp(tot))
            for sol, qk, ch, egc, tot, gc in zip(sols, qks, chunks, egcs, tots, gcs)]


def _unit_triangular_inverse(a_mats, ri, ci, eye):
    c = a_mats[0].shape[0]
    base = 16
    same = (ri // base) == (ci // base)
    pws = [jnp.where(same, -a, 0.0) for a in a_mats]
    invs = [jnp.where(eye, 1.0, 0.0) + x for x in pws]
    for _ in range(int(math.log2(base)) - 1):
        pws = [_bdot(pw, pw) for pw in pws]
        invs = [inv + _bdot(inv, pw) for inv, pw in zip(invs, pws)]
    size = base
    while size < c:
        wider = (ri // (2 * size)) == (ci // (2 * size))
        pick = wider & jnp.logical_not(same)
        inner = [_bdot(jnp.where(pick, a, 0.0), inv) for a, inv in zip(a_mats, invs)]
        invs = [inv - _bdot(inv, e) for inv, e in zip(invs, inner)]
        same, size = wider, 2 * size
    return invs


def _gdn_chunk_step(pre, state):
    u, w, qk, q_in, k_out, e_tot = pre
    v_new = u - _bdot(w, state)
    o = _bdot(q_in, state) + _bdot(qk, v_new)
    state = state * e_tot + _bdot_t(k_out, v_new, _TN)
    return o, state


def _gdn_kernel(q_ref, k_ref, v_ref, z_ref, ab_ref, wq_ref, wk_ref, wv_ref,
                alog_ref, dt_ref, nw_ref, o_ref, qn_ref, kn_ref, vn_ref, acc_ref, *, seq):
    c = GDN_CHUNK
    n = seq // c
    h = pl.program_id(1)

    def prep(i, carry):
        r0 = pl.multiple_of(i * c, c)
        r = pl.ds(r0, c)
        qn_ref[r, :] = _l2norm(_conv_silu_block(q_ref, wq_ref, r0, c, seq)) * (GDN_DK ** -0.5)
        kn_ref[r, :] = _l2norm(_conv_silu_block(k_ref, wk_ref, r0, c, seq))
        vn_ref[r, :] = _conv_silu_block(v_ref, wv_ref, r0, c, seq)
        acc_ref[r, :] = jnp.zeros((c, GDN_DV), F32)
        return carry

    lax.fori_loop(0, n, prep, 0)

    lane = lax.broadcasted_iota(jnp.int32, (1, LANES), 1)
    sel_row = lax.broadcasted_iota(jnp.int32, (LANES, 2 * LANES), 0)
    sel_col = lax.broadcasted_iota(jnp.int32, (LANES, 2 * LANES), 1)

    def gates(r, d):
        x = ab_ref[r, :]
        gall = -jnp.exp(alog_ref[...]) * _softplus(x + dt_ref[...])
        y = jnp.where(lane < 2 * GDN_HEADS, gall, _sigmoid(x))
        g_lane = d * GDN_HEADS + h
        want = jnp.where(sel_col < LANES, g_lane, g_lane + 2 * GDN_HEADS)
        both = _wide_dot_rhs01(y, (sel_row == want).astype(BF16))
        return both[:, :LANES], both[:, LANES:]

    def step(i, states):
        sf, sb = states
        which = []
        for t in range(GDN_UNROLL):
            which += [(i * GDN_UNROLL + t, 0), (n - 1 - i * GDN_UNROLL - t, 1)]
        rows = [pl.ds(pl.multiple_of(j * c, c), c) for j, _ in which]
        gb = [gates(r, d) for r, (_, d) in zip(rows, which)]
        pre = _gdn_prepare([(qn_ref[r, :], kn_ref[r, :], vn_ref[r, :], g, beta, d == 1)
                            for r, (g, beta), (_, d) in zip(rows, gb, which)])
        outs = []
        for t in range(GDN_UNROLL):
            of, sf = _gdn_chunk_step(pre[2 * t], sf)
            ob, sb = _gdn_chunk_step(pre[2 * t + 1], sb)
            outs += [of, ob]
        for r, o in zip(rows, outs):
            acc_ref[r, :] += o
        return sf, sb

    zero = jnp.zeros((GDN_DK, GDN_DV), F32)
    lax.fori_loop(0, n // GDN_UNROLL, step, (zero, zero))

    def finish(i, carry):
        r = pl.ds(pl.multiple_of(i * c, c), c)
        o = acc_ref[r, :]
        o = o * lax.rsqrt(jnp.mean(o * o, axis=-1, keepdims=True) + EPS) * nw_ref[...]
        o_ref[r, :] = (o * _silu(z_ref[r, :])).astype(o_ref.dtype)
        return carry

    lax.fori_loop(0, n, finish, 0)


def _gdn(p1, pab, conv_w, a_log, dt_bias, norm_w, qkv_off, z_off, batch, seq):
    hh = GDN_HEADS
    pad = lambda t: jnp.pad(t.reshape(1, 2 * hh).astype(F32), ((0, 0), (0, LANES - 2 * hh)))
    tok = lambda off: pl.BlockSpec((seq, LANES), lambda b, h, off=off: (b, off + h))
    cw = lambda off: pl.BlockSpec((CONV_WIDTH, LANES), lambda b, h, off=off: (0, off + h))
    row = pl.BlockSpec((1, LANES), lambda b, h: (0, 0))
    return pl.pallas_call(
        functools.partial(_gdn_kernel, seq=seq),
        grid=(batch, hh),
        in_specs=[tok(qkv_off), tok(qkv_off + hh), tok(qkv_off + 2 * hh), tok(z_off),
                  pl.BlockSpec((seq, LANES), lambda b, h: (b, 0)),
                  cw(0), cw(hh), cw(2 * hh), row, row, row],
        out_specs=pl.BlockSpec((seq, GDN_DV), lambda b, h: (b, h)),
        out_shape=jax.ShapeDtypeStruct((batch * seq, hh * GDN_DV), BF16),
        scratch_shapes=[pltpu.VMEM((seq, LANES), F32)] * 4,
        compiler_params=_params("parallel", "parallel"),
        name="gdn",
    )(p1, p1, p1, p1, pab, conv_w, conv_w, conv_w, pad(a_log), pad(dt_bias),
      norm_w.reshape(1, GDN_DV).astype(F32))


def _diff_kernel(q_ref, k_ref, v_ref, cq_ref, sq_ref, ck_ref, sk_ref, lam_ref, nw_ref, o_ref,
                 kr_ref, vb_ref, *, lambda_init):
    d = DIFF_D

    @pl.when(pl.program_id(2) == 0)
    def _():
        for t in range(2):
            kr_ref[t] = _rope(k_ref[:, t * d:(t + 1) * d], ck_ref[...], sk_ref[...]).astype(BF16)
        vb_ref[...] = v_ref[...].astype(BF16)

    lp = lam_ref[...]
    lam = (jnp.exp(jnp.sum(lp[0:1] * lp[1:2], axis=-1, keepdims=True))
           - jnp.exp(jnp.sum(lp[2:3] * lp[3:4], axis=-1, keepdims=True)) + lambda_init)
    tq = q_ref.shape[0]
    n_part = tq // DIFF_ROWS
    halves = [slice(r * DIFF_ROWS, (r + 1) * DIFF_ROWS) for r in range(n_part)]
    scores = []
    for t in range(2):
        q = _rope(q_ref[:, t * d:(t + 1) * d], cq_ref[...], sq_ref[...]) * (d ** -0.5 * LOG2_E)
        q = q.astype(BF16)
        for r in halves:
            scores.append(lax.dot_general(q[r], kr_ref[t], (_NT, ((), ())),
                                          preferred_element_type=F32))
    outs = []
    for s in scores:
        p = jnp.exp2(s - jnp.max(s, axis=-1, keepdims=True))
        l = jnp.sum(p, axis=-1, keepdims=True)
        outs.append(jnp.dot(p.astype(BF16), vb_ref[...], preferred_element_type=F32) / l)
    outs = [jnp.concatenate(outs[0:n_part], axis=0), jnp.concatenate(outs[n_part:], axis=0)]
    o = outs[0] - lam * outs[1]
    o = o * lax.rsqrt(jnp.mean(o * o, axis=-1, keepdims=True) + EPS) * nw_ref[...]
    o_ref[...] = (o * (1.0 - lambda_init)).astype(o_ref.dtype)


def _diff_attention(p2, cos, sin_signed, lam_params, subln_w, lambda_init, q_off, batch, seq, tq):
    nq = seq // tq
    hh = DIFF_HEADS
    return pl.pallas_call(
        functools.partial(_diff_kernel, lambda_init=lambda_init),
        grid=(batch, hh, nq),
        in_specs=[
            pl.BlockSpec((tq, 2 * DIFF_D), lambda b, h, i: (b * nq + i, q_off + h)),
            pl.BlockSpec((seq, 2 * DIFF_D), lambda b, h, i: (b, q_off + hh + h)),
            pl.BlockSpec((seq, DIFF_DV), lambda b, h, i: (b, q_off + 2 * hh + h)),
            pl.BlockSpec((tq, DIFF_D), lambda b, h, i: (b * nq + i, 0)),
            pl.BlockSpec((tq, DIFF_D), lambda b, h, i: (b * nq + i, 0)),
            pl.BlockSpec((seq, DIFF_D), lambda b, h, i: (b, 0)),
            pl.BlockSpec((seq, DIFF_D), lambda b, h, i: (b, 0)),
            pl.BlockSpec((4, DIFF_D), lambda b, h, i: (0, 0)),
            pl.BlockSpec((1, DIFF_DV), lambda b, h, i: (0, 0)),
        ],
        out_specs=pl.BlockSpec((tq, DIFF_DV), lambda b, h, i: (b * nq + i, h)),
        out_shape=jax.ShapeDtypeStruct((batch * seq, hh * DIFF_DV), BF16),
        scratch_shapes=[pltpu.VMEM((2, seq, DIFF_D), BF16), pltpu.VMEM((seq, DIFF_DV), BF16)],
        compiler_params=_params("parallel", "parallel", "arbitrary"),
        name="diff_attention",
    )(p2, p2, p2, cos, sin_signed, cos, sin_signed, lam_params.astype(F32),
      subln_w.reshape(1, DIFF_DV).astype(F32))


def _rope_tables(positions, dim):
    inv = 1.0 / (ROPE_THETA ** (jnp.arange(0, dim, 2, dtype=F32) / dim))
    ang = positions.astype(F32)[..., None] * inv
    ang = jnp.concatenate([ang, ang], axis=-1).reshape(-1, dim)
    sign = jnp.concatenate([-jnp.ones((dim // 2,), F32), jnp.ones((dim // 2,), F32)])
    return jnp.cos(ang), jnp.sin(ang) * sign


def _ffn(x, ln, w_gate, w_up, w_down, lead, d_model, d_ff):
    h = _rmsnorm(x, ln, BF16)
    act = _matmul([(h, 0, d_model, w_gate, lead, 0, 0), (h, 0, d_model, w_up, lead, 0, 0)], [],
                  _ep_swiglu, d_ff, BF16, tm=512, tn=1024, name="ffn_up")
    half = d_ff // 2
    for kb in range(2):
        x = _matmul([(act, kb, half, w_down, lead, kb, 0)], [(x, 0)],
                    functools.partial(_ep_residual, 0.5), d_model, F32,
                    tm=1024, tn=1024, name="ffn_down")
    return x


def kernel(x, p, positions, ln_ffn, ffn_w_gate, ffn_w_up, ffn_w_down, ln_mix, w_in, conv_w,
           gdn_a_log, gdn_dt_bias, gdn_norm_w, diff_lambda, diff_subln_w, w_branch, w_out,
           ln_ple, w_ple_gate, w_ple_proj, final_norm):
    batch, seq, d_model = x.shape
    depth = p.shape[0]
    ple_dim = p.shape[-1]
    d_ff = ffn_w_gate.shape[-1]
    t = batch * seq
    cos, sin_signed = _rope_tables(positions, RET_DK)

    n_ret = RET_HEADS * (2 * RET_DK + 2 * RET_DV)
    n_gdn = GDN_HEADS * (2 * GDN_DK + 2 * GDN_DV)
    n1 = n_ret + n_gdn
    n_ab = 4 * GDN_HEADS
    n_diff = DIFF_HEADS * (4 * DIFF_D + DIFF_DV)
    n2 = n_diff + N_BRANCH * d_model
    tn_in = 1024

    x = x.reshape(t, d_model)
    p = p.reshape(depth, t, ple_dim)
    w_in_t = jnp.swapaxes(w_in, 1, 2)
    for i in range(depth):
        lambda_init = 0.8 - 0.6 * math.exp(-0.3 * i)
        x = _ffn(x, ln_ffn[i, 0], ffn_w_gate, ffn_w_up, ffn_w_down, (i, 0), d_model, d_ff)

        h = _rmsnorm(x, ln_mix[i], BF16)
        p1 = _matmul([(h, 0, d_model, w_in_t, (i,), 0, 0)], [], _ep_identity, n1, F32,
                     tm=1024, tn=tn_in, name="w_in_head", transposed=True)
        pab = _matmul([(h, 0, d_model, w_in_t, (i,), 0, n1 // LANES)], [], _ep_identity, LANES, F32,
                      tm=1024, tn=LANES, name="w_in_gates", transposed=True)
        p2 = _matmul([(h, 0, d_model, w_in_t, (i,), 0, n1 // tn_in)], [], _ep_identity, n2, F32,
                     tm=1024, tn=tn_in, name="w_in_tail", transposed=True, row_shift=n_ab)

        y_ret = _retention(p1, cos, sin_signed, batch, seq)
        y_gdn = _gdn(p1, pab, conv_w[i], gdn_a_log[i], gdn_dt_bias[i], gdn_norm_w[i],
                     n_ret // LANES, (n_ret + GDN_HEADS * (2 * GDN_DK + GDN_DV)) // LANES,
                     batch, seq)
        y_diff = _diff_attention(p2, cos, sin_signed, diff_lambda[i], diff_subln_w[i],
                                 lambda_init, 0, batch, seq, tq=512)

        bw = RET_HEADS * RET_DV
        tn_m = 512
        gate_off = n_diff // tn_m
        merged = _matmul(
            [(y, 0, bw, w_branch, (i, b), 0, 0) for b, y in enumerate((y_ret, y_gdn, y_diff))],
            [(p2, gate_off + b * (d_model // tn_m)) for b in range(N_BRANCH)],
            _ep_merge, d_model, BF16, tm=1024, tn=tn_m, name="merge")
        x = _matmul([(merged, 0, d_model, w_out, (i,), 0, 0)], [(x, 0)],
                    functools.partial(_ep_residual, 1.0), d_model, F32,
                    tm=1024, tn=1024, name="w_out")

        x = _ffn(x, ln_ffn[i, 1], ffn_w_gate, ffn_w_up, ffn_w_down, (i, 1), d_model, d_ff)

        h = _rmsnorm(x, ln_ple[i], BF16)
        x = _matmul([(h, 0, d_model, w_ple_gate, (i,), 0, 0),
                     (p[i].astype(BF16), 0, ple_dim, w_ple_proj, (i,), 0, 0)], [(x, 0)],
                    _ep_ple, d_model, F32, tm=1024, tn=512, name="ple")
    return _rmsnorm(x, final_norm, F32).reshape(batch, seq, d_model)
```

```python
import functools
import math

import jax
import jax.numpy as jnp
from jax import lax
from jax.experimental import pallas as pl
from jax.experimental.pallas import tpu as pltpu

F32 = jnp.float32
BF16 = jnp.bfloat16

EPS = 1e-6
ROPE_THETA = 10000.0
LOG2_E = math.log2(math.e)
LANES = 128
VMEM_LIMIT = 62 * 1024 * 1024

RET_HEADS, RET_DK, RET_DV, RET_CHUNK = 8, 128, 256, 128
RET_UNROLL = 4
GDN_HEADS, GDN_DK, GDN_DV, CONV_WIDTH = 16, 128, 128, 5
GDN_CHUNK = 256
GDN_UNROLL = 4
DIFF_HEADS, DIFF_D, DIFF_DV = 8, 128, 256
DIFF_ROWS = 256
N_BRANCH = 3


def _params(*sem):
    return pltpu.CompilerParams(dimension_semantics=sem, vmem_limit_bytes=VMEM_LIMIT)


def _bdot(a, b):
    return jnp.dot(a.astype(BF16), b.astype(BF16), preferred_element_type=F32)


def _bdot_t(a, b, dims):
    return lax.dot_general(a.astype(BF16), b.astype(BF16), (dims, ((), ())),
                           preferred_element_type=F32)


_NT = ((1,), (1,))
_TN = ((0,), (0,))


def _fdot(a, b):
    return jnp.dot(a, b, preferred_element_type=F32, precision=lax.Precision.HIGHEST)


def _sigmoid(x):
    return 1.0 / (1.0 + jnp.exp(-x))


def _silu(x):
    return x * _sigmoid(x)


def _rms_kernel(x_ref, w_ref, o_ref):
    x = x_ref[...]
    y = x * lax.rsqrt(jnp.mean(x * x, axis=-1, keepdims=True) + EPS)
    o_ref[...] = (y * w_ref[...]).astype(o_ref.dtype)


def _rmsnorm(x, w, out_dtype, tm=256):
    m, d = x.shape
    return pl.pallas_call(
        _rms_kernel,
        grid=(m // tm,),
        in_specs=[pl.BlockSpec((tm, d), lambda i: (i, 0)),
                  pl.BlockSpec((1, d), lambda i: (0, 0))],
        out_specs=pl.BlockSpec((tm, d), lambda i: (i, 0)),
        out_shape=jax.ShapeDtypeStruct((m, d), out_dtype),
        compiler_params=_params("parallel"),
        name="rmsnorm",
    )(x, w.reshape(1, d))


def _mm_kernel(*refs, a_index, n_extra, epilogue, transposed, split, scaled, norm_out, d_norm):
    n_a, n_w = max(a_index) + 1, len(a_index)
    a_refs, refs = refs[:n_a], refs[n_a:]
    w_refs, refs = refs[:n_w * split], refs[n_w * split:]
    e_refs, refs = refs[:n_extra], refs[n_extra:]
    if scaled:
        ssq_ref, refs = refs[0], refs[1:]
    if norm_out:
        ln_ref, o_ref, hs_ref, part_ref = refs[:4]
        scratch = refs[4:]
    else:
        o_ref, scratch = refs[0], refs[1:]
    jj, i = pl.program_id(0), pl.program_id(1)

    slot = jj % 2
    for p, s in enumerate(scratch):
        for c in range(split):
            w = w_refs[p * split + c]
            ch = w.shape[0]
            r0 = pl.multiple_of((i * split + c) * ch, ch)
            s[slot, pl.ds(r0, ch), :] = w[...].astype(BF16)

    @pl.when(jj >= 1)
    def _():
        dims = (_NT if transposed else ((1,), (0,)), ((), ()))
        accs = [lax.dot_general(a_refs[ai][...], s[1 - slot], dims, preferred_element_type=F32)
                for ai, s in zip(a_index, scratch)]
        if scaled:
            ssq = ssq_ref[...]
            tot = ssq[:, 0:1]
            for part in range(1, ssq.shape[1] // LANES):
                tot = tot + ssq[:, part * LANES:part * LANES + 1]
            r = lax.rsqrt(tot * (1.0 / d_norm) + EPS)
            accs = [acc * r if p in scaled else acc for p, acc in enumerate(accs)]
        y = epilogue(accs, [e[...] for e in e_refs])
        o_ref[...] = y.astype(o_ref.dtype)
        if norm_out:
            hs_ref[...] = (y * ln_ref[...]).astype(BF16)
            part_ref[...] = jnp.broadcast_to(jnp.sum(y * y, axis=1, keepdims=True), part_ref.shape)


def _matmul(pairs, extras, epilogue, n, out_dtype, tm, tn, name, transposed=False, row_shift=0,
            ssq=None, scaled=(), next_ln=None):
    m = pairs[0][0].shape[0]
    n_i, n_j = m // tm, n // tn
    split = tn // (n_i * row_shift) if row_shift else 1
    row = lambda jj, i: jnp.where(jj >= 1, i, 0)
    col = lambda jj: jnp.maximum(jj - 1, 0)
    nxt = lambda jj: jnp.minimum(jj, n_j - 1)
    a_specs, a_args, a_keys, a_index, w_specs, scratch = [], [], [], [], [], []
    for (a, acb, k, w, lead, wkb, wco) in pairs:
        key = (id(a), acb, k)
        if key not in a_keys:
            a_keys.append(key)
            a_args.append(a)
            a_specs.append(pl.BlockSpec((tm, k), lambda jj, i, acb=acb: (row(jj, i), acb)))
        a_index.append(a_keys.index(key))
        squeeze = (None,) * len(lead)
        if transposed:
            ch = tn // (n_i * split)
            assert ch % 16 == 0 and row_shift % ch == 0
            for c in range(split):
                w_specs.append(pl.BlockSpec(
                    squeeze + (ch, k),
                    lambda jj, i, lead=lead, wkb=wkb, wco=wco, c=c:
                        (*lead, ((wco + nxt(jj)) * n_i + i) * split + c + row_shift // ch, wkb)))
            scratch.append(pltpu.VMEM((2, tn, k), BF16))
        else:
            ch = k // (n_i * split)
            assert ch % 16 == 0 and not row_shift
            for c in range(split):
                w_specs.append(pl.BlockSpec(
                    squeeze + (ch, tn),
                    lambda jj, i, lead=lead, wkb=wkb, wco=wco, c=c:
                        (*lead, (wkb * n_i + i) * split + c, wco + nxt(jj))))
            scratch.append(pltpu.VMEM((2, k, tn), BF16))
    e_specs = [pl.BlockSpec((tm, tn), lambda jj, i, eco=eco: (row(jj, i), eco + col(jj)))
               for (_, eco) in extras]
    e_args = [e[0] for e in extras]
    if scaled:
        e_specs.append(pl.BlockSpec((tm, ssq.shape[1]), lambda jj, i: (row(jj, i), 0)))
        e_args.append(ssq)
    tile = lambda width: pl.BlockSpec((tm, width), lambda jj, i: (row(jj, i), col(jj)))
    out_specs, out_shape = tile(tn), jax.ShapeDtypeStruct((m, n), out_dtype)
    if next_ln is not None:
        e_specs.append(pl.BlockSpec((1, tn), lambda jj, i: (0, col(jj))))
        e_args.append(next_ln.reshape(1, n).astype(F32))
        out_specs = [out_specs, tile(tn), tile(LANES)]
        out_shape = [out_shape, jax.ShapeDtypeStruct((m, n), BF16),
                     jax.ShapeDtypeStruct((m, n_j * LANES), F32)]
    kern = functools.partial(_mm_kernel, a_index=tuple(a_index), n_extra=len(extras),
                             epilogue=epilogue, transposed=transposed, split=split,
                             scaled=tuple(scaled), norm_out=next_ln is not None,
                             d_norm=pairs[0][2] if scaled else None)
    return pl.pallas_call(
        kern,
        grid=(n_j + 1, n_i),
        in_specs=a_specs + w_specs + e_specs,
        out_specs=out_specs,
        out_shape=out_shape,
        scratch_shapes=scratch,
        compiler_params=_params("arbitrary", "arbitrary"),
        name=name,
    )(*a_args, *[p[3] for p in pairs for _ in range(split)], *e_args)


def _ep_swiglu(accs, ex):
    return _silu(accs[0]) * accs[1]


def _ep_identity(accs, ex):
    return accs[0]


def _ep_residual(scale, accs, ex):
    return ex[0] + scale * accs[0]


def _ep_ple(accs, ex):
    return ex[0] + _sigmoid(accs[0]) * accs[1]


def _ep_merge(accs, ex):
    out = _sigmoid(ex[0]) * accs[0]
    for b in range(1, N_BRANCH):
        out = out + _sigmoid(ex[b]) * accs[b]
    return out


def _rope(x, cos, sin_signed):
    return x * cos + pltpu.roll(x, x.shape[-1] // 2, axis=x.ndim - 1) * sin_signed


def _retention_kernel(q_ref, k_ref, v_ref, g_ref, cos_ref, sin_ref, lg_ref, o_ref,
                      kr_ref, st_ref, *, seq):
    c = RET_CHUNK
    n = seq // c
    lg = lg_ref[0:1, :]
    lgk = lg[:, :RET_DK]
    ri = lax.broadcasted_iota(jnp.int32, (c, RET_DK), 0).astype(F32)
    rj = lax.broadcasted_iota(jnp.int32, (c, c), 1).astype(F32)
    q_dec_f = jnp.exp(lgk * (ri + 1.0))
    k_dec_f = jnp.exp(lgk * (c - 1.0 - ri))
    q_dec_b = jnp.exp(lgk * (c - ri))
    k_dec_b = jnp.exp(lgk * ri)
    intra_decay = jnp.exp(lgk * jnp.abs(ri[:, :c] - rj))
    chunk_decay = jnp.exp(lg * float(c))
    scale = RET_DK ** -0.5

    def rows(i):
        return pl.ds(pl.multiple_of(i * c, c), c)

    u_n = RET_UNROLL

    def fwd(it, state):
        idx = [it * u_n + u for u in range(u_n)]
        rs = [rows(i) for i in idx]
        krs = [_rope(k_ref[r, :], cos_ref[r, :], sin_ref[r, :]) for r in rs]
        for r, kr in zip(rs, krs):
            kr_ref[r, :] = kr
        kvs = [_bdot_t(kr * k_dec_f, v_ref[r, :], _TN) for r, kr in zip(rs, krs)]
        for i, kv in zip(idx, kvs):
            st_ref[i] = state
            state = chunk_decay * state + kv
        return state

    lax.fori_loop(0, n // u_n, fwd, jnp.zeros((RET_DK, RET_DV), F32))

    def bwd(it, state):
        idx = [n - 1 - (it * u_n + u) for u in range(u_n)]
        rs = [rows(i) for i in idx]
        qs = [_rope(q_ref[r, :], cos_ref[r, :], sin_ref[r, :]) * scale for r in rs]
        krs = [kr_ref[r, :] for r in rs]
        vs = [v_ref[r, :] for r in rs]
        scores = [_bdot_t(q, kr, _NT) * intra_decay for q, kr in zip(qs, krs)]
        local = [_bdot(s, v) + _bdot(q * q_dec_f, st_ref[i])
                 for s, v, q, i in zip(scores, vs, qs, idx)]
        kvs = [_bdot_t(kr * k_dec_b, v, _TN) for kr, v in zip(krs, vs)]
        outs = []
        for o, q, kv in zip(local, qs, kvs):
            outs.append(o + _bdot(q * q_dec_b, state))
            state = chunk_decay * state + kv
        for r, o in zip(rs, outs):
            o = o * lax.rsqrt(jnp.mean(o * o, axis=-1, keepdims=True) + EPS)
            o_ref[r, :] = (o * _silu(g_ref[r, :])).astype(o_ref.dtype)
        return state

    lax.fori_loop(0, n // u_n, bwd, jnp.zeros((RET_DK, RET_DV), F32))


def _retention(p1, cos, sin_signed, batch, seq):
    lg = jnp.log(1.0 - 2.0 ** (-5.0 - jnp.arange(RET_HEADS, dtype=F32)))
    lg = jnp.broadcast_to(lg[:, None, None], (RET_HEADS, 8, RET_DV))
    kq, kv = RET_HEADS, (2 * RET_HEADS * RET_DK) // RET_DV
    return pl.pallas_call(
        functools.partial(_retention_kernel, seq=seq),
        grid=(batch, RET_HEADS),
        in_specs=[
            pl.BlockSpec((seq, RET_DK), lambda b, h: (b, h)),
            pl.BlockSpec((seq, RET_DK), lambda b, h: (b, kq + h)),
            pl.BlockSpec((seq, RET_DV), lambda b, h: (b, kv + h)),
            pl.BlockSpec((seq, RET_DV), lambda b, h: (b, kv + RET_HEADS + h)),
            pl.BlockSpec((seq, RET_DK), lambda b, h: (b, 0)),
            pl.BlockSpec((seq, RET_DK), lambda b, h: (b, 0)),
            pl.BlockSpec((None, 8, RET_DV), lambda b, h: (h, 0, 0)),
        ],
        out_specs=pl.BlockSpec((seq, RET_DV), lambda b, h: (b, h)),
        out_shape=jax.ShapeDtypeStruct((batch * seq, RET_HEADS * RET_DV), BF16),
        scratch_shapes=[pltpu.VMEM((seq, RET_DK), F32),
                        pltpu.VMEM((seq // RET_CHUNK, RET_DK, RET_DV), F32)],
        compiler_params=_params("parallel", "parallel"),
        name="retention",
    )(p1, p1, p1, p1, cos, sin_signed, lg)


def _split2(x):
    hi = x.astype(BF16)
    lo = (x - hi.astype(F32)).astype(BF16)
    return hi, lo


def _wide_dot_rhs01(x, sel):
    hi, lo = _split2(x)
    return jnp.dot(jnp.concatenate([hi, lo], axis=1), jnp.concatenate([sel, sel], axis=0),
                   preferred_element_type=F32)


def _wide_dot_lhs01(sel, x):
    hi, lo = _split2(x)
    r = jnp.dot(sel, jnp.concatenate([hi, lo], axis=1), preferred_element_type=F32)
    return r[:, :LANES] + r[:, LANES:]


def _softplus(x):
    return jnp.maximum(x, 0.0) + jnp.log1p(jnp.exp(-jnp.abs(x)))


def _conv_silu_block(x_ref, w_ref, r0, rows, seq):
    mid = x_ref[pl.ds(r0, rows), :]
    top = x_ref[pl.ds(pl.multiple_of(jnp.maximum(r0 - 8, 0), 8), 8), :]
    top = jnp.where(r0 > 0, top, 0.0)
    bot = x_ref[pl.ds(pl.multiple_of(jnp.minimum(r0 + rows, seq - 8), 8), 8), :]
    bot = jnp.where(r0 + rows < seq, bot, 0.0)
    win = jnp.concatenate([top, mid, bot], axis=0)
    half = CONV_WIDTH // 2
    acc = None
    for t in range(CONV_WIDTH):
        sh = win if t == half else pltpu.roll(win, (half - t) % (rows + 16), axis=0)
        term = sh[8:8 + rows, :] * w_ref[t:t + 1, :]
        acc = term if acc is None else acc + term
    return _silu(acc)


def _l2norm(x):
    return x * lax.rsqrt(jnp.sum(x * x, axis=-1, keepdims=True) + EPS)


def _gdn_prepare(chunks):
    c = GDN_CHUNK
    ri = lax.broadcasted_iota(jnp.int32, (c, c), 0)
    ci = lax.broadcasted_iota(jnp.int32, (c, c), 1)
    eye = ri == ci
    incl = {False: ri >= ci, True: ri <= ci}
    strict = {False: ri > ci, True: ri < ci}
    incl_bf = {up: m.astype(BF16) for up, m in incl.items()}
    ups = [ch[5] for ch in chunks]
    gcs = [_wide_dot_lhs01(incl_bf[up], ch[3]) for ch, up in zip(chunks, ups)]
    tots = [gc[0:1, :] if up else gc[c - 1:c, :] for gc, up in zip(gcs, ups)]
    decays = []
    for gc, up in zip(gcs, ups):
        gci = jnp.concatenate([gc, gc], axis=1)
        gcj = jnp.sum(jnp.where(eye, gci, 0.0), axis=0, keepdims=True)
        decays.append(jnp.where(incl[up], jnp.exp(jnp.where(incl[up], gci - gcj, 0.0)), 0.0))
    egcs = [jnp.exp(gc) for gc in gcs]
    kbs = [ch[1] * ch[4] for ch in chunks]
    a_mats = [jnp.where(strict[up], _bdot_t(kb, ch[1], _NT) * dec, 0.0)
              for kb, ch, dec, up in zip(kbs, chunks, decays, ups)]
    qks = [_bdot_t(ch[0], ch[1], _NT) * dec for ch, dec in zip(chunks, decays)]
    invs = _unit_triangular_inverse(a_mats, ri, ci, eye)
    sols = [_bdot(inv, jnp.concatenate([ch[2] * ch[4], kb * egc], axis=1))
            for inv, ch, kb, egc in zip(invs, chunks, kbs, egcs)]
    return [(sol[:, :GDN_DV], sol[:, GDN_DV:], qk, ch[0] * egc, ch[1] * jnp.exp(tot - gc), jnp.exp(tot))
            for sol, qk, ch, egc, tot, gc in zip(sols, qks, chunks, egcs, tots, gcs)]


def _unit_triangular_inverse(a_mats, ri, ci, eye):
    c = a_mats[0].shape[0]
    base = 16
    same = (ri // base) == (ci // base)
    pws = [jnp.where(same, -a, 0.0) for a in a_mats]
    invs = [jnp.where(eye, 1.0, 0.0) + x for x in pws]
    for _ in range(int(math.log2(base)) - 1):
        pws = [_bdot(pw, pw) for pw in pws]
        invs = [inv + _bdot(inv, pw) for inv, pw in zip(invs, pws)]
    size = base
    while size < c:
        wider = (ri // (2 * size)) == (ci // (2 * size))
        pick = wider & jnp.logical_not(same)
        inner = [_bdot(jnp.where(pick, a, 0.0), inv) for a, inv in zip(a_mats, invs)]
        invs = [inv - _bdot(inv, e) for inv, e in zip(invs, inner)]
        same, size = wider, 2 * size
    return invs


def _gdn_chunk_step(pre, state):
    u, w, qk, q_in, k_out, e_tot = pre
    v_new = u - _bdot(w, state)
    o = _bdot(q_in, state) + _bdot(qk, v_new)
    state = state * e_tot + _bdot_t(k_out, v_new, _TN)
    return o, state


def _gdn_kernel(q_ref, k_ref, v_ref, z_ref, ab_ref, wq_ref, wk_ref, wv_ref,
                alog_ref, dt_ref, nw_ref, o_ref, qn_ref, kn_ref, vn_ref, acc_ref, *, seq):
    c = GDN_CHUNK
    n = seq // c
    h = pl.program_id(1)

    def prep(i, carry):
        r0 = pl.multiple_of(i * c, c)
        r = pl.ds(r0, c)
        qn_ref[r, :] = _l2norm(_conv_silu_block(q_ref, wq_ref, r0, c, seq)) * (GDN_DK ** -0.5)
        kn_ref[r, :] = _l2norm(_conv_silu_block(k_ref, wk_ref, r0, c, seq))
        vn_ref[r, :] = _conv_silu_block(v_ref, wv_ref, r0, c, seq)
        acc_ref[r, :] = jnp.zeros((c, GDN_DV), F32)
        return carry

    lax.fori_loop(0, n, prep, 0)

    lane = lax.broadcasted_iota(jnp.int32, (1, LANES), 1)
    sel_row = lax.broadcasted_iota(jnp.int32, (LANES, 2 * LANES), 0)
    sel_col = lax.broadcasted_iota(jnp.int32, (LANES, 2 * LANES), 1)

    def gates(r, d):
        x = ab_ref[r, :]
        gall = -jnp.exp(alog_ref[...]) * _softplus(x + dt_ref[...])
        y = jnp.where(lane < 2 * GDN_HEADS, gall, _sigmoid(x))
        g_lane = d * GDN_HEADS + h
        want = jnp.where(sel_col < LANES, g_lane, g_lane + 2 * GDN_HEADS)
        both = _wide_dot_rhs01(y, (sel_row == want).astype(BF16))
        return both[:, :LANES], both[:, LANES:]

    def step(i, states):
        sf, sb = states
        which = []
        for t in range(GDN_UNROLL):
            which += [(i * GDN_UNROLL + t, 0), (n - 1 - i * GDN_UNROLL - t, 1)]
        rows = [pl.ds(pl.multiple_of(j * c, c), c) for j, _ in which]
        gb = [gates(r, d) for r, (_, d) in zip(rows, which)]
        pre = _gdn_prepare([(qn_ref[r, :], kn_ref[r, :], vn_ref[r, :], g, beta, d == 1)
                            for r, (g, beta), (_, d) in zip(rows, gb, which)])
        outs = []
        for t in range(GDN_UNROLL):
            of, sf = _gdn_chunk_step(pre[2 * t], sf)
            ob, sb = _gdn_chunk_step(pre[2 * t + 1], sb)
            outs += [of, ob]
        for r, o in zip(rows, outs):
            acc_ref[r, :] += o
        return sf, sb

    zero = jnp.zeros((GDN_DK, GDN_DV), F32)
    lax.fori_loop(0, n // GDN_UNROLL, step, (zero, zero))

    def finish(i, carry):
        r = pl.ds(pl.multiple_of(i * c, c), c)
        o = acc_ref[r, :]
        o = o * lax.rsqrt(jnp.mean(o * o, axis=-1, keepdims=True) + EPS) * nw_ref[...]
        o_ref[r, :] = (o * _silu(z_ref[r, :])).astype(o_ref.dtype)
        return carry

    lax.fori_loop(0, n, finish, 0)


def _gdn(p1, pab, conv_w, a_log, dt_bias, norm_w, qkv_off, z_off, batch, seq):
    hh = GDN_HEADS
    pad = lambda t: jnp.pad(t.reshape(1, 2 * hh).astype(F32), ((0, 0), (0, LANES - 2 * hh)))
    tok = lambda off: pl.BlockSpec((seq, LANES), lambda b, h, off=off: (b, off + h))
    cw = lambda off: pl.BlockSpec((CONV_WIDTH, LANES), lambda b, h, off=off: (0, off + h))
    row = pl.BlockSpec((1, LANES), lambda b, h: (0, 0))
    return pl.pallas_call(
        functools.partial(_gdn_kernel, seq=seq),
        grid=(batch, hh),
        in_specs=[tok(qkv_off), tok(qkv_off + hh), tok(qkv_off + 2 * hh), tok(z_off),
                  pl.BlockSpec((seq, LANES), lambda b, h: (b, 0)),
                  cw(0), cw(hh), cw(2 * hh), row, row, row],
        out_specs=pl.BlockSpec((seq, GDN_DV), lambda b, h: (b, h)),
        out_shape=jax.ShapeDtypeStruct((batch * seq, hh * GDN_DV), BF16),
        scratch_shapes=[pltpu.VMEM((seq, LANES), F32)] * 4,
        compiler_params=_params("parallel", "parallel"),
        name="gdn",
    )(p1, p1, p1, p1, pab, conv_w, conv_w, conv_w, pad(a_log), pad(dt_bias),
      norm_w.reshape(1, GDN_DV).astype(F32))


def _diff_kernel(q_ref, k_ref, v_ref, cq_ref, sq_ref, ck_ref, sk_ref, lam_ref, nw_ref, o_ref,
                 kr_ref, vb_ref, *, lambda_init):
    d = DIFF_D

    @pl.when(pl.program_id(2) == 0)
    def _():
        for t in range(2):
            kr_ref[t] = _rope(k_ref[:, t * d:(t + 1) * d], ck_ref[...], sk_ref[...]).astype(BF16)
        vb_ref[...] = v_ref[...].astype(BF16)

    lp = lam_ref[...]
    lam = (jnp.exp(jnp.sum(lp[0:1] * lp[1:2], axis=-1, keepdims=True))
           - jnp.exp(jnp.sum(lp[2:3] * lp[3:4], axis=-1, keepdims=True)) + lambda_init)
    tq = q_ref.shape[0]
    n_part = tq // DIFF_ROWS
    halves = [slice(r * DIFF_ROWS, (r + 1) * DIFF_ROWS) for r in range(n_part)]
    scores = []
    for t in range(2):
        q = _rope(q_ref[:, t * d:(t + 1) * d], cq_ref[...], sq_ref[...]) * (d ** -0.5 * LOG2_E)
        q = q.astype(BF16)
        for r in halves:
            scores.append(lax.dot_general(q[r], kr_ref[t], (_NT, ((), ())),
                                          preferred_element_type=F32))
    outs = []
    for s in scores:
        p = jnp.exp2(s - jnp.max(s, axis=-1, keepdims=True))
        l = jnp.sum(p, axis=-1, keepdims=True)
        outs.append(jnp.dot(p.astype(BF16), vb_ref[...], preferred_element_type=F32) / l)
    outs = [jnp.concatenate(outs[0:n_part], axis=0), jnp.concatenate(outs[n_part:], axis=0)]
    o = outs[0] - lam * outs[1]
    o = o * lax.rsqrt(jnp.mean(o * o, axis=-1, keepdims=True) + EPS) * nw_ref[...]
    o_ref[...] = (o * (1.0 - lambda_init)).astype(o_ref.dtype)


def _diff_attention(p2, cos, sin_signed, lam_params, subln_w, lambda_init, q_off, batch, seq, tq):
    nq = seq // tq
    hh = DIFF_HEADS
    return pl.pallas_call(
        functools.partial(_diff_kernel, lambda_init=lambda_init),
        grid=(batch, hh, nq),
        in_specs=[
            pl.BlockSpec((tq, 2 * DIFF_D), lambda b, h, i: (b * nq + i, q_off + h)),
            pl.BlockSpec((seq, 2 * DIFF_D), lambda b, h, i: (b, q_off + hh + h)),
            pl.BlockSpec((seq, DIFF_DV), lambda b, h, i: (b, q_off + 2 * hh + h)),
            pl.BlockSpec((tq, DIFF_D), lambda b, h, i: (b * nq + i, 0)),
            pl.BlockSpec((tq, DIFF_D), lambda b, h, i: (b * nq + i, 0)),
            pl.BlockSpec((seq, DIFF_D), lambda b, h, i: (b, 0)),
            pl.BlockSpec((seq, DIFF_D), lambda b, h, i: (b, 0)),
            pl.BlockSpec((4, DIFF_D), lambda b, h, i: (0, 0)),
            pl.BlockSpec((1, DIFF_DV), lambda b, h, i: (0, 0)),
        ],
        out_specs=pl.BlockSpec((tq, DIFF_DV), lambda b, h, i: (b * nq + i, h)),
        out_shape=jax.ShapeDtypeStruct((batch * seq, hh * DIFF_DV), BF16),
        scratch_shapes=[pltpu.VMEM((2, seq, DIFF_D), BF16), pltpu.VMEM((seq, DIFF_DV), BF16)],
        compiler_params=_params("parallel", "parallel", "arbitrary"),
        name="diff_attention",
    )(p2, p2, p2, cos, sin_signed, cos, sin_signed, lam_params.astype(F32),
      subln_w.reshape(1, DIFF_DV).astype(F32))


def _rope_tables(positions, dim):
    inv = 1.0 / (ROPE_THETA ** (jnp.arange(0, dim, 2, dtype=F32) / dim))
    ang = positions.astype(F32)[..., None] * inv
    ang = jnp.concatenate([ang, ang], axis=-1).reshape(-1, dim)
    sign = jnp.concatenate([-jnp.ones((dim // 2,), F32), jnp.ones((dim // 2,), F32)])
    return jnp.cos(ang), jnp.sin(ang) * sign


def _ffn(x, h, ssq, w_gate, w_up, w_down, lead, d_model, d_ff, next_ln):
    sc = dict(ssq=ssq, scaled=(0, 1)) if ssq is not None else {}
    act = _matmul([(h, 0, d_model, w_gate, lead, 0, 0), (h, 0, d_model, w_up, lead, 0, 0)], [],
                  _ep_swiglu, d_ff, BF16, tm=512, tn=1024, name="ffn_up", **sc)
    half = d_ff // 2
    x = _matmul([(act, 0, half, w_down, lead, 0, 0)], [(x, 0)],
                functools.partial(_ep_residual, 0.5), d_model, F32,
                tm=1024, tn=1024, name="ffn_down")
    return _matmul([(act, 1, half, w_down, lead, 1, 0)], [(x, 0)],
                   functools.partial(_ep_residual, 0.5), d_model, F32,
                   tm=512, tn=1024, name="ffn_down_norm", next_ln=next_ln)


def kernel(x, p, positions, ln_ffn, ffn_w_gate, ffn_w_up, ffn_w_down, ln_mix, w_in, conv_w,
           gdn_a_log, gdn_dt_bias, gdn_norm_w, diff_lambda, diff_subln_w, w_branch, w_out,
           ln_ple, w_ple_gate, w_ple_proj, final_norm):
    batch, seq, d_model = x.shape
    depth = p.shape[0]
    ple_dim = p.shape[-1]
    d_ff = ffn_w_gate.shape[-1]
    t = batch * seq
    cos, sin_signed = _rope_tables(positions, RET_DK)

    n_ret = RET_HEADS * (2 * RET_DK + 2 * RET_DV)
    n_gdn = GDN_HEADS * (2 * GDN_DK + 2 * GDN_DV)
    n1 = n_ret + n_gdn
    n_ab = 4 * GDN_HEADS
    n_diff = DIFF_HEADS * (4 * DIFF_D + DIFF_DV)
    n2 = n_diff + N_BRANCH * d_model
    tn_in = 1024

    x = x.reshape(t, d_model)
    p = p.reshape(depth, t, ple_dim)
    w_in_t = jnp.swapaxes(w_in, 1, 2)
    h, ssq = _rmsnorm(x, ln_ffn[0, 0], BF16), None
    for i in range(depth):
        lambda_init = 0.8 - 0.6 * math.exp(-0.3 * i)
        x, h, ssq = _ffn(x, h, ssq, ffn_w_gate, ffn_w_up, ffn_w_down, (i, 0), d_model, d_ff, ln_mix[i])

        sc = dict(ssq=ssq, scaled=(0,))
        p1 = _matmul([(h, 0, d_model, w_in_t, (i,), 0, 0)], [], _ep_identity, n1, F32,
                     tm=1024, tn=tn_in, name="w_in_head", transposed=True, **sc)
        pab = _matmul([(h, 0, d_model, w_in_t, (i,), 0, n1 // LANES)], [], _ep_identity, LANES, F32,
                      tm=1024, tn=LANES, name="w_in_gates", transposed=True, **sc)
        p2 = _matmul([(h, 0, d_model, w_in_t, (i,), 0, n1 // tn_in)], [], _ep_identity, n2, F32,
                     tm=1024, tn=tn_in, name="w_in_tail", transposed=True, row_shift=n_ab, **sc)

        y_ret = _retention(p1, cos, sin_signed, batch, seq)
        y_gdn = _gdn(p1, pab, conv_w[i], gdn_a_log[i], gdn_dt_bias[i], gdn_norm_w[i],
                     n_ret // LANES, (n_ret + GDN_HEADS * (2 * GDN_DK + GDN_DV)) // LANES,
                     batch, seq)
        y_diff = _diff_attention(p2, cos, sin_signed, diff_lambda[i], diff_subln_w[i],
                                 lambda_init, 0, batch, seq, tq=512)

        bw = RET_HEADS * RET_DV
        tn_m = 512
        gate_off = n_diff // tn_m
        merged = _matmul(
            [(y, 0, bw, w_branch, (i, b), 0, 0) for b, y in enumerate((y_ret, y_gdn, y_diff))],
            [(p2, gate_off + b * (d_model // tn_m)) for b in range(N_BRANCH)],
            _ep_merge, d_model, BF16, tm=1024, tn=tn_m, name="merge")
        x, h, ssq = _matmul([(merged, 0, d_model, w_out, (i,), 0, 0)], [(x, 0)],
                            functools.partial(_ep_residual, 1.0), d_model, F32,
                            tm=512, tn=1024, name="w_out", next_ln=ln_ffn[i, 1])

        x, h, ssq = _ffn(x, h, ssq, ffn_w_gate, ffn_w_up, ffn_w_down, (i, 1), d_model, d_ff, ln_ple[i])

        last = i + 1 == depth
        out = _matmul([(h, 0, d_model, w_ple_gate, (i,), 0, 0),
                       (p[i].astype(BF16), 0, ple_dim, w_ple_proj, (i,), 0, 0)], [(x, 0)],
                      _ep_ple, d_model, F32, tm=1024, tn=512, name="ple", ssq=ssq, scaled=(0,),
                      next_ln=None if last else ln_ffn[i + 1, 0])
        x, h, ssq = (out, None, None) if last else out
    return _rmsnorm(x, final_norm, F32).reshape(batch, seq, d_model)
```

```python
import functools
import math

import jax
import jax.numpy as jnp
from jax import lax
from jax.experimental import pallas as pl
from jax.experimental.pallas import tpu as pltpu

F32 = jnp.float32
BF16 = jnp.bfloat16

EPS = 1e-6
ROPE_THETA = 10000.0
LOG2_E = math.log2(math.e)
LANES = 128
VMEM_LIMIT = 62 * 1024 * 1024

RET_HEADS, RET_DK, RET_DV, RET_CHUNK = 8, 128, 256, 128
RET_UNROLL = 4
GDN_HEADS, GDN_DK, GDN_DV, CONV_WIDTH = 16, 128, 128, 5
GDN_CHUNK = 256
GDN_UNROLL = 4
DIFF_HEADS, DIFF_D, DIFF_DV = 8, 128, 256
DIFF_ROWS = 256
DIFF_TQ = 512
N_BRANCH = 3

TILES = {
    "ffn_up": (512, 1024), "ffn_down": (1024, 1024), "ffn_down_norm": (512, 1024),
    "w_in_head": (1024, 1024), "w_in_gates": (1024, LANES), "w_in_tail": (1024, 1024),
    "merge": (512, 1024), "w_out": (512, 1024), "ple": (512, 1024),
}


def _params(*sem):
    return pltpu.CompilerParams(dimension_semantics=sem, vmem_limit_bytes=VMEM_LIMIT)


def _bdot(a, b):
    return jnp.dot(a.astype(BF16), b.astype(BF16), preferred_element_type=F32)


def _bdot_t(a, b, dims):
    return lax.dot_general(a.astype(BF16), b.astype(BF16), (dims, ((), ())),
                           preferred_element_type=F32)


_NT = ((1,), (1,))
_TN = ((0,), (0,))


def _sigmoid(x):
    return 1.0 / (1.0 + jnp.exp(-x))


def _silu(x):
    return x * _sigmoid(x)


def _rms_kernel(x_ref, w_ref, o_ref):
    x = x_ref[...]
    y = x * lax.rsqrt(jnp.mean(x * x, axis=-1, keepdims=True) + EPS)
    o_ref[...] = (y * w_ref[...]).astype(o_ref.dtype)


def _rmsnorm(x, w, out_dtype, tm=256):
    m, d = x.shape
    return pl.pallas_call(
        _rms_kernel,
        grid=(m // tm,),
        in_specs=[pl.BlockSpec((tm, d), lambda i: (i, 0)),
                  pl.BlockSpec((1, d), lambda i: (0, 0))],
        out_specs=pl.BlockSpec((tm, d), lambda i: (i, 0)),
        out_shape=jax.ShapeDtypeStruct((m, d), out_dtype),
        compiler_params=_params("parallel"),
        name="rmsnorm",
    )(x, w.reshape(1, d))


def _mm_kernel(*refs, a_index, n_extra, epilogue, transposed, split, scaled, norm_out, d_norm):
    n_a, n_w = max(a_index) + 1, len(a_index)
    a_refs, refs = refs[:n_a], refs[n_a:]
    w_refs, refs = refs[:n_w * split], refs[n_w * split:]
    e_refs, refs = refs[:n_extra], refs[n_extra:]
    if scaled:
        ssq_ref, refs = refs[0], refs[1:]
    if norm_out:
        ln_ref, o_ref, hs_ref, part_ref = refs[:4]
        scratch = refs[4:]
    else:
        o_ref, scratch = refs[0], refs[1:]
    jj, i = pl.program_id(0), pl.program_id(1)

    slot = jj % 2
    for p, s in enumerate(scratch):
        for c in range(split):
            w = w_refs[p * split + c]
            ch = w.shape[0]
            r0 = pl.multiple_of((i * split + c) * ch, ch)
            s[slot, pl.ds(r0, ch), :] = w[...].astype(BF16)

    @pl.when(jj >= 1)
    def _():
        dims = (_NT if transposed else ((1,), (0,)), ((), ()))
        accs = [lax.dot_general(a_refs[ai][...], s[1 - slot], dims, preferred_element_type=F32)
                for ai, s in zip(a_index, scratch)]
        if scaled:
            ssq = ssq_ref[...]
            tot = ssq[:, 0:1]
            for part in range(1, ssq.shape[1] // LANES):
                tot = tot + ssq[:, part * LANES:part * LANES + 1]
            r = lax.rsqrt(tot * (1.0 / d_norm) + EPS)
            accs = [acc * r if p in scaled else acc for p, acc in enumerate(accs)]
        y = epilogue(accs, [e[...] for e in e_refs])
        o_ref[...] = y.astype(o_ref.dtype)
        if norm_out:
            hs_ref[...] = (y * ln_ref[...]).astype(BF16)
            part_ref[...] = jnp.broadcast_to(jnp.sum(y * y, axis=1, keepdims=True), part_ref.shape)


def _matmul(pairs, extras, epilogue, n, out_dtype, name, transposed=False, row_shift=0,
            ssq=None, scaled=(), next_ln=None):
    m = pairs[0][0].shape[0]
    tm, tn = TILES[name]
    n_i, n_j = m // tm, n // tn
    split = tn // (n_i * row_shift) if row_shift else 1
    row = lambda jj, i: jnp.where(jj >= 1, i, 0)
    col = lambda jj: jnp.maximum(jj - 1, 0)
    nxt = lambda jj: jnp.minimum(jj, n_j - 1)
    a_specs, a_args, a_keys, a_index, w_specs, scratch = [], [], [], [], [], []
    for (a, acb, k, w, lead, wkb, wco) in pairs:
        key = (id(a), acb, k)
        if key not in a_keys:
            a_keys.append(key)
            a_args.append(a)
            a_specs.append(pl.BlockSpec((tm, k), lambda jj, i, acb=acb: (row(jj, i), acb)))
        a_index.append(a_keys.index(key))
        squeeze = (None,) * len(lead)
        if transposed:
            ch = tn // (n_i * split)
            assert ch % 16 == 0 and row_shift % ch == 0
            for c in range(split):
                w_specs.append(pl.BlockSpec(
                    squeeze + (ch, k),
                    lambda jj, i, lead=lead, wkb=wkb, wco=wco, c=c:
                        (*lead, ((wco + nxt(jj)) * n_i + i) * split + c + row_shift // ch, wkb)))
            scratch.append(pltpu.VMEM((2, tn, k), BF16))
        else:
            ch = k // (n_i * split)
            assert ch % 16 == 0 and not row_shift
            for c in range(split):
                w_specs.append(pl.BlockSpec(
                    squeeze + (ch, tn),
                    lambda jj, i, lead=lead, wkb=wkb, wco=wco, c=c:
                        (*lead, (wkb * n_i + i) * split + c, wco + nxt(jj))))
            scratch.append(pltpu.VMEM((2, k, tn), BF16))
    e_specs = [pl.BlockSpec((tm, tn), lambda jj, i, eco=eco: (row(jj, i), eco + col(jj)))
               for (_, eco) in extras]
    e_args = [e[0] for e in extras]
    if scaled:
        e_specs.append(pl.BlockSpec((tm, ssq.shape[1]), lambda jj, i: (row(jj, i), 0)))
        e_args.append(ssq)
    tile = lambda width: pl.BlockSpec((tm, width), lambda jj, i: (row(jj, i), col(jj)))
    out_specs, out_shape = tile(tn), jax.ShapeDtypeStruct((m, n), out_dtype)
    if next_ln is not None:
        e_specs.append(pl.BlockSpec((1, tn), lambda jj, i: (0, col(jj))))
        e_args.append(next_ln.reshape(1, n).astype(F32))
        out_specs = [out_specs, tile(tn), tile(LANES)]
        out_shape = [out_shape, jax.ShapeDtypeStruct((m, n), BF16),
                     jax.ShapeDtypeStruct((m, n_j * LANES), F32)]
    kern = functools.partial(_mm_kernel, a_index=tuple(a_index), n_extra=len(extras),
                             epilogue=epilogue, transposed=transposed, split=split,
                             scaled=tuple(scaled), norm_out=next_ln is not None,
                             d_norm=pairs[0][2] if scaled else None)
    return pl.pallas_call(
        kern,
        grid=(n_j + 1, n_i),
        in_specs=a_specs + w_specs + e_specs,
        out_specs=out_specs,
        out_shape=out_shape,
        scratch_shapes=scratch,
        compiler_params=_params("arbitrary", "arbitrary"),
        name=name,
    )(*a_args, *[p[3] for p in pairs for _ in range(split)], *e_args)


def _ep_swiglu(accs, ex):
    return _silu(accs[0]) * accs[1]


def _ep_identity(accs, ex):
    return accs[0]


def _ep_residual(scale, accs, ex):
    return ex[0] + scale * accs[0]


def _ep_ple(accs, ex):
    return ex[0] + _sigmoid(accs[0]) * accs[1]


def _ep_merge(accs, ex):
    out = _sigmoid(ex[0]) * accs[0]
    for b in range(1, N_BRANCH):
        out = out + _sigmoid(ex[b]) * accs[b]
    return out


def _rope(x, cos, sin_signed):
    return x * cos + pltpu.roll(x, x.shape[-1] // 2, axis=x.ndim - 1) * sin_signed


def _retention_kernel(q_ref, k_ref, v_ref, g_ref, cos_ref, sin_ref, lg_ref, o_ref,
                      kr_ref, st_ref, *, seq):
    c = RET_CHUNK
    n = seq // c
    lg = lg_ref[0:1, :]
    lgk = lg[:, :RET_DK]
    ri = lax.broadcasted_iota(jnp.int32, (c, RET_DK), 0).astype(F32)
    rj = lax.broadcasted_iota(jnp.int32, (c, c), 1).astype(F32)
    q_dec_f = jnp.exp(lgk * (ri + 1.0))
    k_dec_f = jnp.exp(lgk * (c - 1.0 - ri))
    q_dec_b = jnp.exp(lgk * (c - ri))
    k_dec_b = jnp.exp(lgk * ri)
    intra_decay = jnp.exp(lgk * jnp.abs(ri[:, :c] - rj))
    chunk_decay = jnp.exp(lg * float(c))
    scale = RET_DK ** -0.5

    def rows(i):
        return pl.ds(pl.multiple_of(i * c, c), c)

    u_n = RET_UNROLL

    def fwd(it, state):
        idx = [it * u_n + u for u in range(u_n)]
        rs = [rows(i) for i in idx]
        krs = [_rope(k_ref[r, :], cos_ref[r, :], sin_ref[r, :]) for r in rs]
        for r, kr in zip(rs, krs):
            kr_ref[r, :] = kr
        kvs = [_bdot_t(kr * k_dec_f, v_ref[r, :], _TN) for r, kr in zip(rs, krs)]
        for i, kv in zip(idx, kvs):
            st_ref[i] = state
            state = chunk_decay * state + kv
        return state

    lax.fori_loop(0, n // u_n, fwd, jnp.zeros((RET_DK, RET_DV), F32))

    def bwd(it, state):
        idx = [n - 1 - (it * u_n + u) for u in range(u_n)]
        rs = [rows(i) for i in idx]
        qs = [_rope(q_ref[r, :], cos_ref[r, :], sin_ref[r, :]) * scale for r in rs]
        krs = [kr_ref[r, :] for r in rs]
        vs = [v_ref[r, :] for r in rs]
        scores = [_bdot_t(q, kr, _NT) * intra_decay for q, kr in zip(qs, krs)]
        local = [_bdot(s, v) + _bdot(q * q_dec_f, st_ref[i])
                 for s, v, q, i in zip(scores, vs, qs, idx)]
        kvs = [_bdot_t(kr * k_dec_b, v, _TN) for kr, v in zip(krs, vs)]
        outs = []
        for o, q, kv in zip(local, qs, kvs):
            outs.append(o + _bdot(q * q_dec_b, state))
            state = chunk_decay * state + kv
        for r, o in zip(rs, outs):
            o = o * lax.rsqrt(jnp.mean(o * o, axis=-1, keepdims=True) + EPS)
            o_ref[r, :] = (o * _silu(g_ref[r, :])).astype(o_ref.dtype)
        return state

    lax.fori_loop(0, n // u_n, bwd, jnp.zeros((RET_DK, RET_DV), F32))


def _retention(p1, cos, sin_signed, batch, seq):
    lg = jnp.log(1.0 - 2.0 ** (-5.0 - jnp.arange(RET_HEADS, dtype=F32)))
    lg = jnp.broadcast_to(lg[:, None, None], (RET_HEADS, 8, RET_DV))
    kq, kv = RET_HEADS, (2 * RET_HEADS * RET_DK) // RET_DV
    return pl.pallas_call(
        functools.partial(_retention_kernel, seq=seq),
        grid=(batch, RET_HEADS),
        in_specs=[
            pl.BlockSpec((seq, RET_DK), lambda b, h: (b, h)),
            pl.BlockSpec((seq, RET_DK), lambda b, h: (b, kq + h)),
            pl.BlockSpec((seq, RET_DV), lambda b, h: (b, kv + h)),
            pl.BlockSpec((seq, RET_DV), lambda b, h: (b, kv + RET_HEADS + h)),
            pl.BlockSpec((seq, RET_DK), lambda b, h: (b, 0)),
            pl.BlockSpec((seq, RET_DK), lambda b, h: (b, 0)),
            pl.BlockSpec((None, 8, RET_DV), lambda b, h: (h, 0, 0)),
        ],
        out_specs=pl.BlockSpec((seq, RET_DV), lambda b, h: (b, h)),
        out_shape=jax.ShapeDtypeStruct((batch * seq, RET_HEADS * RET_DV), BF16),
        scratch_shapes=[pltpu.VMEM((seq, RET_DK), F32),
                        pltpu.VMEM((seq // RET_CHUNK, RET_DK, RET_DV), F32)],
        compiler_params=_params("parallel", "parallel"),
        name="retention",
    )(p1, p1, p1, p1, cos, sin_signed, lg)


def _split2(x):
    hi = x.astype(BF16)
    lo = (x - hi.astype(F32)).astype(BF16)
    return hi, lo


def _wide_dot_rhs01(x, sel):
    hi, lo = _split2(x)
    return jnp.dot(jnp.concatenate([hi, lo], axis=1), jnp.concatenate([sel, sel], axis=0),
                   preferred_element_type=F32)


def _wide_dot_lhs01(sel, x):
    hi, lo = _split2(x)
    r = jnp.dot(sel, jnp.concatenate([hi, lo], axis=1), preferred_element_type=F32)
    return r[:, :LANES] + r[:, LANES:]


def _softplus(x):
    return jnp.maximum(x, 0.0) + jnp.log1p(jnp.exp(-jnp.abs(x)))


def _conv_silu_block(x_ref, w_ref, r0, rows, seq):
    mid = x_ref[pl.ds(r0, rows), :]
    top = x_ref[pl.ds(pl.multiple_of(jnp.maximum(r0 - 8, 0), 8), 8), :]
    top = jnp.where(r0 > 0, top, 0.0)
    bot = x_ref[pl.ds(pl.multiple_of(jnp.minimum(r0 + rows, seq - 8), 8), 8), :]
    bot = jnp.where(r0 + rows < seq, bot, 0.0)
    win = jnp.concatenate([top, mid, bot], axis=0)
    half = CONV_WIDTH // 2
    acc = None
    for t in range(CONV_WIDTH):
        sh = win if t == half else pltpu.roll(win, (half - t) % (rows + 16), axis=0)
        term = sh[8:8 + rows, :] * w_ref[t:t + 1, :]
        acc = term if acc is None else acc + term
    return _silu(acc)


def _l2norm(x):
    return x * lax.rsqrt(jnp.sum(x * x, axis=-1, keepdims=True) + EPS)


def _gdn_prepare(chunks):
    c = GDN_CHUNK
    ri = lax.broadcasted_iota(jnp.int32, (c, c), 0)
    ci = lax.broadcasted_iota(jnp.int32, (c, c), 1)
    eye = ri == ci
    incl = {False: ri >= ci, True: ri <= ci}
    strict = {False: ri > ci, True: ri < ci}
    incl_bf = {up: m.astype(BF16) for up, m in incl.items()}
    ups = [ch[5] for ch in chunks]
    gcs = [_wide_dot_lhs01(incl_bf[up], ch[3]) for ch, up in zip(chunks, ups)]
    tots = [gc[0:1, :] if up else gc[c - 1:c, :] for gc, up in zip(gcs, ups)]
    decays = []
    for gc, up in zip(gcs, ups):
        gci = jnp.concatenate([gc, gc], axis=1)
        gcj = jnp.sum(jnp.where(eye, gci, 0.0), axis=0, keepdims=True)
        decays.append(jnp.where(incl[up], jnp.exp(gci - gcj), 0.0))
    egcs = [jnp.exp(gc) for gc in gcs]
    kbs = [ch[1] * ch[4] for ch in chunks]
    a_mats = [jnp.where(strict[up], _bdot_t(kb, ch[1], _NT) * dec, 0.0)
              for kb, ch, dec, up in zip(kbs, chunks, decays, ups)]
    qks = [_bdot_t(ch[0], ch[1], _NT) * dec for ch, dec in zip(chunks, decays)]
    invs = _unit_triangular_inverse(a_mats, ri, ci, eye)
    sols = [_bdot(inv, jnp.concatenate([ch[2] * ch[4], kb * egc], axis=1))
            for inv, ch, kb, egc in zip(invs, chunks, kbs, egcs)]
    return [(sol[:, :GDN_DV], sol[:, GDN_DV:], qk, ch[0] * egc, ch[1] * jnp.exp(tot - gc), jnp.exp(tot))
            for sol, qk, ch, egc, tot, gc in zip(sols, qks, chunks, egcs, tots, gcs)]


def _unit_triangular_inverse(a_mats, ri, ci, eye):
    c = a_mats[0].shape[0]
    base = 16
    same = (ri // base) == (ci // base)
    pws = [jnp.where(same, -a, 0.0) for a in a_mats]
    invs = [jnp.where(eye, 1.0, 0.0) + x for x in pws]
    for _ in range(int(math.log2(base)) - 1):
        pws = [_bdot(pw, pw) for pw in pws]
        invs = [inv + _bdot(inv, pw) for inv, pw in zip(invs, pws)]
    size = base
    while size < c:
        wider = (ri // (2 * size)) == (ci // (2 * size))
        pick = wider & jnp.logical_not(same)
        inner = [_bdot(jnp.where(pick, a, 0.0), inv) for a, inv in zip(a_mats, invs)]
        invs = [inv - _bdot(inv, e) for inv, e in zip(invs, inner)]
        same, size = wider, 2 * size
    return invs


def _gdn_chunk_step(pre, state):
    u, w, qk, q_in, k_out, e_tot = pre
    v_new = u - _bdot(w, state)
    o = _bdot(q_in, state) + _bdot(qk, v_new)
    state = state * e_tot + _bdot_t(k_out, v_new, _TN)
    return o, state


def _gdn_kernel(q_ref, k_ref, v_ref, z_ref, ab_ref, wq_ref, wk_ref, wv_ref,
                alog_ref, dt_ref, nw_ref, o_ref, qn_ref, kn_ref, vn_ref, acc_ref, *, seq):
    c = GDN_CHUNK
    n = seq // c
    h = pl.program_id(1)

    def prep(i, carry):
        r0 = pl.multiple_of(i * c, c)
        r = pl.ds(r0, c)
        qn_ref[r, :] = _l2norm(_conv_silu_block(q_ref, wq_ref, r0, c, seq)) * (GDN_DK ** -0.5)
        kn_ref[r, :] = _l2norm(_conv_silu_block(k_ref, wk_ref, r0, c, seq))
        vn_ref[r, :] = _conv_silu_block(v_ref, wv_ref, r0, c, seq)
        acc_ref[r, :] = jnp.zeros((c, GDN_DV), F32)
        return carry

    lax.fori_loop(0, n, prep, 0)

    lane = lax.broadcasted_iota(jnp.int32, (1, LANES), 1)
    sel_row = lax.broadcasted_iota(jnp.int32, (LANES, 2 * LANES), 0)
    sel_col = lax.broadcasted_iota(jnp.int32, (LANES, 2 * LANES), 1)

    def gates(r, d):
        x = ab_ref[r, :]
        gall = -jnp.exp(alog_ref[...]) * _softplus(x + dt_ref[...])
        y = jnp.where(lane < 2 * GDN_HEADS, gall, _sigmoid(x))
        g_lane = d * GDN_HEADS + h
        want = jnp.where(sel_col < LANES, g_lane, g_lane + 2 * GDN_HEADS)
        both = _wide_dot_rhs01(y, (sel_row == want).astype(BF16))
        return both[:, :LANES], both[:, LANES:]

    def step(i, states):
        sf, sb = states
        which = []
        for t in range(GDN_UNROLL):
            which += [(i * GDN_UNROLL + t, 0), (n - 1 - i * GDN_UNROLL - t, 1)]
        rows = [pl.ds(pl.multiple_of(j * c, c), c) for j, _ in which]
        gb = [gates(r, d) for r, (_, d) in zip(rows, which)]
        pre = _gdn_prepare([(qn_ref[r, :], kn_ref[r, :], vn_ref[r, :], g, beta, d == 1)
                            for r, (g, beta), (_, d) in zip(rows, gb, which)])
        outs = []
        for t in range(GDN_UNROLL):
            of, sf = _gdn_chunk_step(pre[2 * t], sf)
            ob, sb = _gdn_chunk_step(pre[2 * t + 1], sb)
            outs += [of, ob]
        for r, o in zip(rows, outs):
            acc_ref[r, :] += o
        return sf, sb

    zero = jnp.zeros((GDN_DK, GDN_DV), F32)
    lax.fori_loop(0, n // GDN_UNROLL, step, (zero, zero))

    def finish(i, carry):
        r = pl.ds(pl.multiple_of(i * c, c), c)
        o = acc_ref[r, :]
        o = o * lax.rsqrt(jnp.mean(o * o, axis=-1, keepdims=True) + EPS) * nw_ref[...]
        o_ref[r, :] = (o * _silu(z_ref[r, :])).astype(o_ref.dtype)
        return carry

    lax.fori_loop(0, n, finish, 0)


def _gdn(p1, pab, conv_w, a_log, dt_bias, norm_w, qkv_off, z_off, batch, seq):
    hh = GDN_HEADS
    pad = lambda t: jnp.pad(t.reshape(1, 2 * hh).astype(F32), ((0, 0), (0, LANES - 2 * hh)))
    tok = lambda off: pl.BlockSpec((seq, LANES), lambda b, h, off=off: (b, off + h))
    cw = lambda off: pl.BlockSpec((CONV_WIDTH, LANES), lambda b, h, off=off: (0, off + h))
    row = pl.BlockSpec((1, LANES), lambda b, h: (0, 0))
    return pl.pallas_call(
        functools.partial(_gdn_kernel, seq=seq),
        grid=(batch, hh),
        in_specs=[tok(qkv_off), tok(qkv_off + hh), tok(qkv_off + 2 * hh), tok(z_off),
                  pl.BlockSpec((seq, LANES), lambda b, h: (b, 0)),
                  cw(0), cw(hh), cw(2 * hh), row, row, row],
        out_specs=pl.BlockSpec((seq, GDN_DV), lambda b, h: (b, h)),
        out_shape=jax.ShapeDtypeStruct((batch * seq, hh * GDN_DV), BF16),
        scratch_shapes=[pltpu.VMEM((seq, LANES), F32)] * 4,
        compiler_params=_params("parallel", "parallel"),
        name="gdn",
    )(p1, p1, p1, p1, pab, conv_w, conv_w, conv_w, pad(a_log), pad(dt_bias),
      norm_w.reshape(1, GDN_DV).astype(F32))


def _diff_kernel(q_ref, k_ref, v_ref, cq_ref, sq_ref, ck_ref, sk_ref, lam_ref, nw_ref, o_ref,
                 kr_ref, vb_ref, *, lambda_init):
    d = DIFF_D

    @pl.when(pl.program_id(2) == 0)
    def _():
        for t in range(2):
            kr_ref[t] = _rope(k_ref[:, t * d:(t + 1) * d], ck_ref[...], sk_ref[...]).astype(BF16)
        vb_ref[...] = v_ref[...].astype(BF16)

    lp = lam_ref[...]
    lam = (jnp.exp(jnp.sum(lp[0:1] * lp[1:2], axis=-1, keepdims=True))
           - jnp.exp(jnp.sum(lp[2:3] * lp[3:4], axis=-1, keepdims=True)) + lambda_init)
    tq = q_ref.shape[0]
    n_part = tq // DIFF_ROWS
    halves = [slice(r * DIFF_ROWS, (r + 1) * DIFF_ROWS) for r in range(n_part)]
    scores = []
    for t in range(2):
        q = _rope(q_ref[:, t * d:(t + 1) * d], cq_ref[...], sq_ref[...]) * (d ** -0.5 * LOG2_E)
        q = q.astype(BF16)
        for r in halves:
            scores.append(lax.dot_general(q[r], kr_ref[t], (_NT, ((), ())),
                                          preferred_element_type=F32))
    outs = []
    for s in scores:
        p = jnp.exp2(s - jnp.max(s, axis=-1, keepdims=True))
        l = jnp.sum(p, axis=-1, keepdims=True)
        outs.append(jnp.dot(p.astype(BF16), vb_ref[...], preferred_element_type=F32) / l)
    outs = [jnp.concatenate(outs[0:n_part], axis=0), jnp.concatenate(outs[n_part:], axis=0)]
    o = outs[0] - lam * outs[1]
    o = o * lax.rsqrt(jnp.mean(o * o, axis=-1, keepdims=True) + EPS) * nw_ref[...]
    o_ref[...] = (o * (1.0 - lambda_init)).astype(o_ref.dtype)


def _diff_attention(p2, cos, sin_signed, lam_params, subln_w, lambda_init, q_off, batch, seq, tq):
    nq = seq // tq
    hh = DIFF_HEADS
    return pl.pallas_call(
        functools.partial(_diff_kernel, lambda_init=lambda_init),
        grid=(batch, hh, nq),
        in_specs=[
            pl.BlockSpec((tq, 2 * DIFF_D), lambda b, h, i: (b * nq + i, q_off + h)),
            pl.BlockSpec((seq, 2 * DIFF_D), lambda b, h, i: (b, q_off + hh + h)),
            pl.BlockSpec((seq, DIFF_DV), lambda b, h, i: (b, q_off + 2 * hh + h)),
            pl.BlockSpec((tq, DIFF_D), lambda b, h, i: (b * nq + i, 0)),
            pl.BlockSpec((tq, DIFF_D), lambda b, h, i: (b * nq + i, 0)),
            pl.BlockSpec((seq, DIFF_D), lambda b, h, i: (b, 0)),
            pl.BlockSpec((seq, DIFF_D), lambda b, h, i: (b, 0)),
            pl.BlockSpec((4, DIFF_D), lambda b, h, i: (0, 0)),
            pl.BlockSpec((1, DIFF_DV), lambda b, h, i: (0, 0)),
        ],
        out_specs=pl.BlockSpec((tq, DIFF_DV), lambda b, h, i: (b * nq + i, h)),
        out_shape=jax.ShapeDtypeStruct((batch * seq, hh * DIFF_DV), BF16),
        scratch_shapes=[pltpu.VMEM((2, seq, DIFF_D), BF16), pltpu.VMEM((seq, DIFF_DV), BF16)],
        compiler_params=_params("parallel", "parallel", "arbitrary"),
        name="diff_attention",
    )(p2, p2, p2, cos, sin_signed, cos, sin_signed, lam_params.astype(F32),
      subln_w.reshape(1, DIFF_DV).astype(F32))


def _rope_tables(positions, dim):
    inv = 1.0 / (ROPE_THETA ** (jnp.arange(0, dim, 2, dtype=F32) / dim))
    ang = positions.astype(F32)[..., None] * inv
    ang = jnp.concatenate([ang, ang], axis=-1).reshape(-1, dim)
    sign = jnp.concatenate([-jnp.ones((dim // 2,), F32), jnp.ones((dim // 2,), F32)])
    return jnp.cos(ang), jnp.sin(ang) * sign


def _ffn(x, h, ssq, w_gate, w_up, w_down, lead, d_model, d_ff, next_ln):
    sc = dict(ssq=ssq, scaled=(0, 1)) if ssq is not None else {}
    act = _matmul([(h, 0, d_model, w_gate, lead, 0, 0), (h, 0, d_model, w_up, lead, 0, 0)], [],
                  _ep_swiglu, d_ff, BF16, name="ffn_up", **sc)
    half = d_ff // 2
    x = _matmul([(act, 0, half, w_down, lead, 0, 0)], [(x, 0)],
                functools.partial(_ep_residual, 0.5), d_model, F32,
                name="ffn_down")
    return _matmul([(act, 1, half, w_down, lead, 1, 0)], [(x, 0)],
                   functools.partial(_ep_residual, 0.5), d_model, F32,
                   name="ffn_down_norm", next_ln=next_ln)


def kernel(x, p, positions, ln_ffn, ffn_w_gate, ffn_w_up, ffn_w_down, ln_mix, w_in, conv_w,
           gdn_a_log, gdn_dt_bias, gdn_norm_w, diff_lambda, diff_subln_w, w_branch, w_out,
           ln_ple, w_ple_gate, w_ple_proj, final_norm):
    batch, seq, d_model = x.shape
    depth = p.shape[0]
    ple_dim = p.shape[-1]
    d_ff = ffn_w_gate.shape[-1]
    t = batch * seq
    cos, sin_signed = _rope_tables(positions, RET_DK)

    n_ret = RET_HEADS * (2 * RET_DK + 2 * RET_DV)
    n_gdn = GDN_HEADS * (2 * GDN_DK + 2 * GDN_DV)
    n1 = n_ret + n_gdn
    n_ab = 4 * GDN_HEADS
    n_diff = DIFF_HEADS * (4 * DIFF_D + DIFF_DV)
    n2 = n_diff + N_BRANCH * d_model
    tn_in = TILES["w_in_tail"][1]

    x = x.reshape(t, d_model)
    p = p.reshape(depth, t, ple_dim)
    w_in_t = jnp.swapaxes(w_in, 1, 2)
    h, ssq = _rmsnorm(x, ln_ffn[0, 0], BF16), None
    for i in range(depth):
        lambda_init = 0.8 - 0.6 * math.exp(-0.3 * i)
        x, h, ssq = _ffn(x, h, ssq, ffn_w_gate, ffn_w_up, ffn_w_down, (i, 0), d_model, d_ff, ln_mix[i])

        sc = dict(ssq=ssq, scaled=(0,))
        p1 = _matmul([(h, 0, d_model, w_in_t, (i,), 0, 0)], [], _ep_identity, n1, F32,
                     name="w_in_head", transposed=True, **sc)
        pab = _matmul([(h, 0, d_model, w_in_t, (i,), 0, n1 // LANES)], [], _ep_identity, LANES, F32,
                      name="w_in_gates", transposed=True, **sc)
        p2 = _matmul([(h, 0, d_model, w_in_t, (i,), 0, n1 // tn_in)], [], _ep_identity, n2, F32,
                     name="w_in_tail", transposed=True, row_shift=n_ab, **sc)

        y_ret = _retention(p1, cos, sin_signed, batch, seq)
        y_gdn = _gdn(p1, pab, conv_w[i], gdn_a_log[i], gdn_dt_bias[i], gdn_norm_w[i],
                     n_ret // LANES, (n_ret + GDN_HEADS * (2 * GDN_DK + GDN_DV)) // LANES,
                     batch, seq)
        y_diff = _diff_attention(p2, cos, sin_signed, diff_lambda[i], diff_subln_w[i],
                                 lambda_init, 0, batch, seq, tq=DIFF_TQ)

        bw = RET_HEADS * RET_DV
        tn_m = TILES["merge"][1]
        gate_off = n_diff // tn_m
        merged = _matmul(
            [(y, 0, bw, w_branch, (i, b), 0, 0) for b, y in enumerate((y_ret, y_gdn, y_diff))],
            [(p2, gate_off + b * (d_model // tn_m)) for b in range(N_BRANCH)],
            _ep_merge, d_model, BF16, name="merge")
        x, h, ssq = _matmul([(merged, 0, d_model, w_out, (i,), 0, 0)], [(x, 0)],
                            functools.partial(_ep_residual, 1.0), d_model, F32,
                            name="w_out", next_ln=ln_ffn[i, 1])

        x, h, ssq = _ffn(x, h, ssq, ffn_w_gate, ffn_w_up, ffn_w_down, (i, 1), d_model, d_ff, ln_ple[i])

        last = i + 1 == depth
        out = _matmul([(h, 0, d_model, w_ple_gate, (i,), 0, 0),
                       (p[i].astype(BF16), 0, ple_dim, w_ple_proj, (i,), 0, 0)], [(x, 0)],
                      _ep_ple, d_model, F32, name="ple", ssq=ssq, scaled=(0,),
                      next_ln=None if last else ln_ffn[i + 1, 0])
        x, h, ssq = (out, None, None) if last else out
    return _rmsnorm(x, final_norm, F32).reshape(batch, seq, d_model)
```

```python
import functools
import math

import jax
import jax.numpy as jnp
from jax import lax
from jax.experimental import pallas as pl
from jax.experimental.pallas import tpu as pltpu

F32 = jnp.float32
BF16 = jnp.bfloat16

EPS = 1e-6
ROPE_THETA = 10000.0
LOG2_E = math.log2(math.e)
LANES = 128
VMEM_LIMIT = 62 * 1024 * 1024

RET_HEADS, RET_DK, RET_DV, RET_CHUNK = 8, 128, 256, 128
RET_UNROLL = 4
GDN_HEADS, GDN_DK, GDN_DV, CONV_WIDTH = 16, 128, 128, 5
GDN_CHUNK = 256
GDN_UNROLL = 4
DIFF_HEADS, DIFF_D, DIFF_DV = 8, 128, 256
DIFF_ROWS = 256
DIFF_TQ = 512
N_BRANCH = 3

TILES = {
    "ffn_up": (512, 1024), "ffn_down": (1024, 1024), "ffn_down_norm": (512, 1024),
    "w_in_head": (1024, 1024), "w_in_gates": (1024, LANES), "w_in_tail": (1024, 1024),
    "merge": (512, 1024), "w_out": (512, 1024), "ple": (512, 1024),
}


def _params(*sem):
    return pltpu.CompilerParams(dimension_semantics=sem, vmem_limit_bytes=VMEM_LIMIT)


def _bdot(a, b):
    return jnp.dot(a.astype(BF16), b.astype(BF16), preferred_element_type=F32)


def _bdot_t(a, b, dims):
    return lax.dot_general(a.astype(BF16), b.astype(BF16), (dims, ((), ())),
                           preferred_element_type=F32)


_NT = ((1,), (1,))
_TN = ((0,), (0,))


def _sigmoid(x):
    return 1.0 / (1.0 + jnp.exp(-x))


def _silu(x):
    return x * _sigmoid(x)


def _rms_kernel(x_ref, w_ref, o_ref):
    x = x_ref[...]
    y = x * lax.rsqrt(jnp.mean(x * x, axis=-1, keepdims=True) + EPS)
    o_ref[...] = (y * w_ref[...]).astype(o_ref.dtype)


def _rmsnorm(x, w, out_dtype, tm=256):
    m, d = x.shape
    return pl.pallas_call(
        _rms_kernel,
        grid=(m // tm,),
        in_specs=[pl.BlockSpec((tm, d), lambda i: (i, 0)),
                  pl.BlockSpec((1, d), lambda i: (0, 0))],
        out_specs=pl.BlockSpec((tm, d), lambda i: (i, 0)),
        out_shape=jax.ShapeDtypeStruct((m, d), out_dtype),
        compiler_params=_params("parallel"),
        name="rmsnorm",
    )(x, w.reshape(1, d))


def _mm_kernel(*refs, a_index, n_extra, epilogue, transposed, split, scaled, norm_out, d_norm):
    n_a, n_w = max(a_index) + 1, len(a_index)
    a_refs, refs = refs[:n_a], refs[n_a:]
    w_refs, refs = refs[:n_w * split], refs[n_w * split:]
    e_refs, refs = refs[:n_extra], refs[n_extra:]
    if scaled:
        ssq_ref, refs = refs[0], refs[1:]
    if norm_out:
        ln_ref, o_ref, hs_ref, part_ref = refs[:4]
        scratch = refs[4:]
    else:
        o_ref, scratch = refs[0], refs[1:]
    jj, i = pl.program_id(0), pl.program_id(1)

    slot = jj % 2
    for p, s in enumerate(scratch):
        for c in range(split):
            w = w_refs[p * split + c]
            ch = w.shape[0]
            r0 = pl.multiple_of((i * split + c) * ch, ch)
            s[slot, pl.ds(r0, ch), :] = w[...].astype(BF16)

    @pl.when(jj >= 1)
    def _():
        dims = (_NT if transposed else ((1,), (0,)), ((), ()))
        accs = [lax.dot_general(a_refs[ai][...], s[1 - slot], dims, preferred_element_type=F32)
                for ai, s in zip(a_index, scratch)]
        if scaled:
            ssq = ssq_ref[...]
            tot = ssq[:, 0:1]
            for part in range(1, ssq.shape[1] // LANES):
                tot = tot + ssq[:, part * LANES:part * LANES + 1]
            r = lax.rsqrt(tot * (1.0 / d_norm) + EPS)
            accs = [acc * r if p in scaled else acc for p, acc in enumerate(accs)]
        y = epilogue(accs, [e[...] for e in e_refs])
        o_ref[...] = y.astype(o_ref.dtype)
        if norm_out:
            hs_ref[...] = (y * ln_ref[...]).astype(BF16)
            part_ref[...] = jnp.broadcast_to(jnp.sum(y * y, axis=1, keepdims=True), part_ref.shape)


def _matmul(pairs, extras, epilogue, n, out_dtype, name, transposed=False, row_shift=0,
            ssq=None, scaled=(), next_ln=None):
    m = pairs[0][0].shape[0]
    tm, tn = TILES[name]
    n_i, n_j = m // tm, n // tn
    split = tn // (n_i * row_shift) if row_shift else 1
    row = lambda jj, i: jnp.where(jj >= 1, i, 0)
    col = lambda jj: jnp.maximum(jj - 1, 0)
    nxt = lambda jj: jnp.minimum(jj, n_j - 1)
    a_specs, a_args, a_keys, a_index, w_specs, scratch = [], [], [], [], [], []
    for (a, acb, k, w, lead, wkb, wco) in pairs:
        key = (id(a), acb, k)
        if key not in a_keys:
            a_keys.append(key)
            a_args.append(a)
            a_specs.append(pl.BlockSpec((tm, k), lambda jj, i, acb=acb: (row(jj, i), acb)))
        a_index.append(a_keys.index(key))
        squeeze = (None,) * len(lead)
        if transposed:
            ch = tn // (n_i * split)
            assert ch % 16 == 0 and row_shift % ch == 0
            for c in range(split):
                w_specs.append(pl.BlockSpec(
                    squeeze + (ch, k),
                    lambda jj, i, lead=lead, wkb=wkb, wco=wco, c=c:
                        (*lead, ((wco + nxt(jj)) * n_i + i) * split + c + row_shift // ch, wkb)))
            scratch.append(pltpu.VMEM((2, tn, k), BF16))
        else:
            ch = k // (n_i * split)
            assert ch % 16 == 0 and not row_shift
            for c in range(split):
                w_specs.append(pl.BlockSpec(
                    squeeze + (ch, tn),
                    lambda jj, i, lead=lead, wkb=wkb, wco=wco, c=c:
                        (*lead, (wkb * n_i + i) * split + c, wco + nxt(jj))))
            scratch.append(pltpu.VMEM((2, k, tn), BF16))
    e_specs = [pl.BlockSpec((tm, tn), lambda jj, i, eco=eco: (row(jj, i), eco + col(jj)))
               for (_, eco) in extras]
    e_args = [e[0] for e in extras]
    if scaled:
        e_specs.append(pl.BlockSpec((tm, ssq.shape[1]), lambda jj, i: (row(jj, i), 0)))
        e_args.append(ssq)
    tile = lambda width: pl.BlockSpec((tm, width), lambda jj, i: (row(jj, i), col(jj)))
    out_specs, out_shape = tile(tn), jax.ShapeDtypeStruct((m, n), out_dtype)
    if next_ln is not None:
        e_specs.append(pl.BlockSpec((1, tn), lambda jj, i: (0, col(jj))))
        e_args.append(next_ln.reshape(1, n).astype(F32))
        out_specs = [out_specs, tile(tn), tile(LANES)]
        out_shape = [out_shape, jax.ShapeDtypeStruct((m, n), BF16),
                     jax.ShapeDtypeStruct((m, n_j * LANES), F32)]
    kern = functools.partial(_mm_kernel, a_index=tuple(a_index), n_extra=len(extras),
                             epilogue=epilogue, transposed=transposed, split=split,
                             scaled=tuple(scaled), norm_out=next_ln is not None,
                             d_norm=pairs[0][2] if scaled else None)
    return pl.pallas_call(
        kern,
        grid=(n_j + 1, n_i),
        in_specs=a_specs + w_specs + e_specs,
        out_specs=out_specs,
        out_shape=out_shape,
        scratch_shapes=scratch,
        compiler_params=_params("arbitrary", "arbitrary"),
        name=name,
    )(*a_args, *[p[3] for p in pairs for _ in range(split)], *e_args)


def _ep_swiglu(accs, ex):
    return _silu(accs[0]) * accs[1]


def _ep_identity(accs, ex):
    return accs[0]


def _ep_residual(scale, accs, ex):
    return ex[0] + scale * accs[0]


def _ep_ple(accs, ex):
    return ex[0] + _sigmoid(accs[0]) * accs[1]


def _ep_merge(accs, ex):
    out = _sigmoid(ex[0].astype(F32)) * accs[0]
    for b in range(1, N_BRANCH):
        out = out + _sigmoid(ex[b].astype(F32)) * accs[b]
    return out


def _rope(x, cos, sin_signed):
    return x * cos + pltpu.roll(x, x.shape[-1] // 2, axis=x.ndim - 1) * sin_signed


def _retention_kernel(q_ref, k_ref, v_ref, g_ref, cos_ref, sin_ref, lg_ref, o_ref,
                      kr_ref, st_ref, *, seq):
    c = RET_CHUNK
    n = seq // c
    lg = lg_ref[0:1, :]
    lgk = lg[:, :RET_DK]
    ri = lax.broadcasted_iota(jnp.int32, (c, RET_DK), 0).astype(F32)
    rj = lax.broadcasted_iota(jnp.int32, (c, c), 1).astype(F32)
    q_dec_f = jnp.exp(lgk * (ri + 1.0))
    k_dec_f = jnp.exp(lgk * (c - 1.0 - ri))
    q_dec_b = jnp.exp(lgk * (c - ri))
    k_dec_b = jnp.exp(lgk * ri)
    intra_decay = jnp.exp(lgk * jnp.abs(ri[:, :c] - rj))
    chunk_decay = jnp.exp(lg * float(c))
    scale = RET_DK ** -0.5

    def rows(i):
        return pl.ds(pl.multiple_of(i * c, c), c)

    u_n = RET_UNROLL

    def fwd(it, state):
        idx = [it * u_n + u for u in range(u_n)]
        rs = [rows(i) for i in idx]
        krs = [_rope(k_ref[r, :], cos_ref[r, :], sin_ref[r, :]) for r in rs]
        for r, kr in zip(rs, krs):
            kr_ref[r, :] = kr
        kvs = [_bdot_t(kr * k_dec_f, v_ref[r, :], _TN) for r, kr in zip(rs, krs)]
        for i, kv in zip(idx, kvs):
            st_ref[i] = state
            state = chunk_decay * state + kv
        return state

    lax.fori_loop(0, n // u_n, fwd, jnp.zeros((RET_DK, RET_DV), F32))

    def bwd(it, state):
        idx = [n - 1 - (it * u_n + u) for u in range(u_n)]
        rs = [rows(i) for i in idx]
        qs = [_rope(q_ref[r, :], cos_ref[r, :], sin_ref[r, :]) * scale for r in rs]
        krs = [kr_ref[r, :] for r in rs]
        vs = [v_ref[r, :] for r in rs]
        scores = [_bdot_t(q, kr, _NT) * intra_decay for q, kr in zip(qs, krs)]
        local = [_bdot(s, v) + _bdot(q * q_dec_f, st_ref[i])
                 for s, v, q, i in zip(scores, vs, qs, idx)]
        kvs = [_bdot_t(kr * k_dec_b, v, _TN) for kr, v in zip(krs, vs)]
        outs = []
        for o, q, kv in zip(local, qs, kvs):
            outs.append(o + _bdot(q * q_dec_b, state))
            state = chunk_decay * state + kv
        for r, o in zip(rs, outs):
            o = o * lax.rsqrt(jnp.mean(o * o, axis=-1, keepdims=True) + EPS)
            o_ref[r, :] = (o * _silu(g_ref[r, :])).astype(o_ref.dtype)
        return state

    lax.fori_loop(0, n // u_n, bwd, jnp.zeros((RET_DK, RET_DV), F32))


def _retention(p1, cos, sin_signed, batch, seq):
    lg = jnp.log(1.0 - 2.0 ** (-5.0 - jnp.arange(RET_HEADS, dtype=F32)))
    lg = jnp.broadcast_to(lg[:, None, None], (RET_HEADS, 8, RET_DV))
    kq, kv = RET_HEADS, (2 * RET_HEADS * RET_DK) // RET_DV
    return pl.pallas_call(
        functools.partial(_retention_kernel, seq=seq),
        grid=(batch, RET_HEADS),
        in_specs=[
            pl.BlockSpec((seq, RET_DK), lambda b, h: (b, h)),
            pl.BlockSpec((seq, RET_DK), lambda b, h: (b, kq + h)),
            pl.BlockSpec((seq, RET_DV), lambda b, h: (b, kv + h)),
            pl.BlockSpec((seq, RET_DV), lambda b, h: (b, kv + RET_HEADS + h)),
            pl.BlockSpec((seq, RET_DK), lambda b, h: (b, 0)),
            pl.BlockSpec((seq, RET_DK), lambda b, h: (b, 0)),
            pl.BlockSpec((None, 8, RET_DV), lambda b, h: (h, 0, 0)),
        ],
        out_specs=pl.BlockSpec((seq, RET_DV), lambda b, h: (b, h)),
        out_shape=jax.ShapeDtypeStruct((batch * seq, RET_HEADS * RET_DV), BF16),
        scratch_shapes=[pltpu.VMEM((seq, RET_DK), F32),
                        pltpu.VMEM((seq // RET_CHUNK, RET_DK, RET_DV), F32)],
        compiler_params=_params("parallel", "parallel"),
        name="retention",
    )(p1, p1, p1, p1, cos, sin_signed, lg)


def _split2(x):
    hi = x.astype(BF16)
    lo = (x - hi.astype(F32)).astype(BF16)
    return hi, lo


def _wide_dot_rhs01(x, sel):
    hi, lo = _split2(x)
    return jnp.dot(jnp.concatenate([hi, lo], axis=1), jnp.concatenate([sel, sel], axis=0),
                   preferred_element_type=F32)


def _wide_dot_lhs01(sel, x):
    hi, lo = _split2(x)
    r = jnp.dot(sel, jnp.concatenate([hi, lo], axis=1), preferred_element_type=F32)
    return r[:, :LANES] + r[:, LANES:]


def _softplus(x):
    return jnp.maximum(x, 0.0) + jnp.log1p(jnp.exp(-jnp.abs(x)))


def _conv_silu_block(x_ref, w_ref, r0, rows, seq):
    mid = x_ref[pl.ds(r0, rows), :]
    top = x_ref[pl.ds(pl.multiple_of(jnp.maximum(r0 - 8, 0), 8), 8), :]
    top = jnp.where(r0 > 0, top, 0.0)
    bot = x_ref[pl.ds(pl.multiple_of(jnp.minimum(r0 + rows, seq - 8), 8), 8), :]
    bot = jnp.where(r0 + rows < seq, bot, 0.0)
    win = jnp.concatenate([top, mid, bot], axis=0)
    half = CONV_WIDTH // 2
    acc = None
    for t in range(CONV_WIDTH):
        sh = win if t == half else pltpu.roll(win, (half - t) % (rows + 16), axis=0)
        term = sh[8:8 + rows, :] * w_ref[t:t + 1, :]
        acc = term if acc is None else acc + term
    return _silu(acc)


def _l2norm(x):
    return x * lax.rsqrt(jnp.sum(x * x, axis=-1, keepdims=True) + EPS)


def _gdn_prepare(chunks):
    c = GDN_CHUNK
    ri = lax.broadcasted_iota(jnp.int32, (c, c), 0)
    ci = lax.broadcasted_iota(jnp.int32, (c, c), 1)
    eye = ri == ci
    incl = {False: ri >= ci, True: ri <= ci}
    strict = {False: ri > ci, True: ri < ci}
    incl_bf = {up: m.astype(BF16) for up, m in incl.items()}
    ups = [ch[5] for ch in chunks]
    gcs = [_wide_dot_lhs01(incl_bf[up], ch[3]) for ch, up in zip(chunks, ups)]
    tots = [gc[0:1, :] if up else gc[c - 1:c, :] for gc, up in zip(gcs, ups)]
    decays = []
    for gc, up in zip(gcs, ups):
        gci = jnp.concatenate([gc, gc], axis=1)
        gcj = jnp.sum(jnp.where(eye, gci, 0.0), axis=0, keepdims=True)
        decays.append(jnp.where(incl[up], jnp.exp(gci - gcj), 0.0))
    egcs = [jnp.exp(gc) for gc in gcs]
    kbs = [ch[1] * ch[4] for ch in chunks]
    a_mats = [jnp.where(strict[up], _bdot_t(kb, ch[1], _NT) * dec, 0.0)
              for kb, ch, dec, up in zip(kbs, chunks, decays, ups)]
    qks = [_bdot_t(ch[0], ch[1], _NT) * dec for ch, dec in zip(chunks, decays)]
    invs = _unit_triangular_inverse(a_mats, ri, ci, eye)
    sols = [_bdot(inv, jnp.concatenate([ch[2] * ch[4], kb * egc], axis=1))
            for inv, ch, kb, egc in zip(invs, chunks, kbs, egcs)]
    return [(sol[:, :GDN_DV], sol[:, GDN_DV:], qk, ch[0] * egc, ch[1] * jnp.exp(tot - gc), jnp.exp(tot))
            for sol, qk, ch, egc, tot, gc in zip(sols, qks, chunks, egcs, tots, gcs)]


def _unit_triangular_inverse(a_mats, ri, ci, eye):
    c = a_mats[0].shape[0]
    base = 16
    same = (ri // base) == (ci // base)
    pws = [jnp.where(same, -a, 0.0) for a in a_mats]
    invs = [jnp.where(eye, 1.0, 0.0) + x for x in pws]
    for _ in range(int(math.log2(base)) - 1):
        pws = [_bdot(pw, pw) for pw in pws]
        invs = [inv + _bdot(inv, pw) for inv, pw in zip(invs, pws)]
    size = base
    while size < c:
        wider = (ri // (2 * size)) == (ci // (2 * size))
        pick = wider & jnp.logical_not(same)
        inner = [_bdot(jnp.where(pick, a, 0.0), inv) for a, inv in zip(a_mats, invs)]
        invs = [inv - _bdot(inv, e) for inv, e in zip(invs, inner)]
        same, size = wider, 2 * size
    return invs


def _gdn_chunk_step(pre, state):
    u, w, qk, q_in, k_out, e_tot = pre
    v_new = u - _bdot(w, state)
    o = _bdot(q_in, state) + _bdot(qk, v_new)
    state = state * e_tot + _bdot_t(k_out, v_new, _TN)
    return o, state


def _gdn_kernel(q_ref, k_ref, v_ref, z_ref, ab_ref, wq_ref, wk_ref, wv_ref,
                alog_ref, dt_ref, nw_ref, o_ref, qn_ref, kn_ref, vn_ref, acc_ref, *, seq):
    c = GDN_CHUNK
    n = seq // c
    h = pl.program_id(1)

    def prep(i, carry):
        r0 = pl.multiple_of(i * c, c)
        r = pl.ds(r0, c)
        qn_ref[r, :] = _l2norm(_conv_silu_block(q_ref, wq_ref, r0, c, seq)) * (GDN_DK ** -0.5)
        kn_ref[r, :] = _l2norm(_conv_silu_block(k_ref, wk_ref, r0, c, seq))
        vn_ref[r, :] = _conv_silu_block(v_ref, wv_ref, r0, c, seq)
        acc_ref[r, :] = jnp.zeros((c, GDN_DV), F32)
        return carry

    lax.fori_loop(0, n, prep, 0)

    lane = lax.broadcasted_iota(jnp.int32, (1, LANES), 1)
    sel_row = lax.broadcasted_iota(jnp.int32, (LANES, 2 * LANES), 0)
    sel_col = lax.broadcasted_iota(jnp.int32, (LANES, 2 * LANES), 1)

    def gates(r, d):
        x = ab_ref[r, :]
        gall = -jnp.exp(alog_ref[...]) * _softplus(x + dt_ref[...])
        y = jnp.where(lane < 2 * GDN_HEADS, gall, _sigmoid(x))
        g_lane = d * GDN_HEADS + h
        want = jnp.where(sel_col < LANES, g_lane, g_lane + 2 * GDN_HEADS)
        both = _wide_dot_rhs01(y, (sel_row == want).astype(BF16))
        return both[:, :LANES], both[:, LANES:]

    def step(i, states):
        sf, sb = states
        which = []
        for t in range(GDN_UNROLL):
            which += [(i * GDN_UNROLL + t, 0), (n - 1 - i * GDN_UNROLL - t, 1)]
        rows = [pl.ds(pl.multiple_of(j * c, c), c) for j, _ in which]
        gb = [gates(r, d) for r, (_, d) in zip(rows, which)]
        pre = _gdn_prepare([(qn_ref[r, :], kn_ref[r, :], vn_ref[r, :], g, beta, d == 1)
                            for r, (g, beta), (_, d) in zip(rows, gb, which)])
        outs = []
        for t in range(GDN_UNROLL):
            of, sf = _gdn_chunk_step(pre[2 * t], sf)
            ob, sb = _gdn_chunk_step(pre[2 * t + 1], sb)
            outs += [of, ob]
        for r, o in zip(rows, outs):
            acc_ref[r, :] += o
        return sf, sb

    zero = jnp.zeros((GDN_DK, GDN_DV), F32)
    lax.fori_loop(0, n // GDN_UNROLL, step, (zero, zero))

    def finish(i, carry):
        r = pl.ds(pl.multiple_of(i * c, c), c)
        o = acc_ref[r, :]
        o = o * lax.rsqrt(jnp.mean(o * o, axis=-1, keepdims=True) + EPS) * nw_ref[...]
        o_ref[r, :] = (o * _silu(z_ref[r, :])).astype(o_ref.dtype)
        return carry

    lax.fori_loop(0, n, finish, 0)


def _gdn(p1, pab, conv_w, a_log, dt_bias, norm_w, qkv_off, z_off, batch, seq):
    hh = GDN_HEADS
    pad = lambda t: jnp.pad(t.reshape(1, 2 * hh).astype(F32), ((0, 0), (0, LANES - 2 * hh)))
    tok = lambda off: pl.BlockSpec((seq, LANES), lambda b, h, off=off: (b, off + h))
    cw = lambda off: pl.BlockSpec((CONV_WIDTH, LANES), lambda b, h, off=off: (0, off + h))
    row = pl.BlockSpec((1, LANES), lambda b, h: (0, 0))
    return pl.pallas_call(
        functools.partial(_gdn_kernel, seq=seq),
        grid=(batch, hh),
        in_specs=[tok(qkv_off), tok(qkv_off + hh), tok(qkv_off + 2 * hh), tok(z_off),
                  pl.BlockSpec((seq, LANES), lambda b, h: (b, 0)),
                  cw(0), cw(hh), cw(2 * hh), row, row, row],
        out_specs=pl.BlockSpec((seq, GDN_DV), lambda b, h: (b, h)),
        out_shape=jax.ShapeDtypeStruct((batch * seq, hh * GDN_DV), BF16),
        scratch_shapes=[pltpu.VMEM((seq, LANES), F32)] * 4,
        compiler_params=_params("parallel", "parallel"),
        name="gdn",
    )(p1, p1, p1, p1, pab, conv_w, conv_w, conv_w, pad(a_log), pad(dt_bias),
      norm_w.reshape(1, GDN_DV).astype(F32))


def _diff_kernel(q_ref, k_ref, v_ref, cq_ref, sq_ref, ck_ref, sk_ref, lam_ref, nw_ref, o_ref,
                 kr_ref, vb_ref, *, lambda_init):
    d = DIFF_D

    @pl.when(pl.program_id(2) == 0)
    def _():
        for t in range(2):
            kr_ref[t] = _rope(k_ref[:, t * d:(t + 1) * d].astype(F32), ck_ref[...], sk_ref[...]).astype(BF16)
        vb_ref[...] = v_ref[...].astype(BF16)

    lp = lam_ref[...]
    lam = (jnp.exp(jnp.sum(lp[0:1] * lp[1:2], axis=-1, keepdims=True))
           - jnp.exp(jnp.sum(lp[2:3] * lp[3:4], axis=-1, keepdims=True)) + lambda_init)
    tq = q_ref.shape[0]
    n_part = tq // DIFF_ROWS
    halves = [slice(r * DIFF_ROWS, (r + 1) * DIFF_ROWS) for r in range(n_part)]
    scores = []
    for t in range(2):
        q = _rope(q_ref[:, t * d:(t + 1) * d].astype(F32), cq_ref[...], sq_ref[...]) * (d ** -0.5 * LOG2_E)
        q = q.astype(BF16)
        for r in halves:
            scores.append(lax.dot_general(q[r], kr_ref[t], (_NT, ((), ())),
                                          preferred_element_type=F32))
    outs = []
    for s in scores:
        p = jnp.exp2(s - jnp.max(s, axis=-1, keepdims=True))
        l = jnp.sum(p, axis=-1, keepdims=True)
        outs.append(jnp.dot(p.astype(BF16), vb_ref[...], preferred_element_type=F32) / l)
    outs = [jnp.concatenate(outs[0:n_part], axis=0), jnp.concatenate(outs[n_part:], axis=0)]
    o = outs[0] - lam * outs[1]
    o = o * lax.rsqrt(jnp.mean(o * o, axis=-1, keepdims=True) + EPS) * nw_ref[...]
    o_ref[...] = (o * (1.0 - lambda_init)).astype(o_ref.dtype)


def _diff_attention(p2, cos, sin_signed, lam_params, subln_w, lambda_init, q_off, batch, seq, tq):
    nq = seq // tq
    hh = DIFF_HEADS
    return pl.pallas_call(
        functools.partial(_diff_kernel, lambda_init=lambda_init),
        grid=(batch, hh, nq),
        in_specs=[
            pl.BlockSpec((tq, 2 * DIFF_D), lambda b, h, i: (b * nq + i, q_off + h)),
            pl.BlockSpec((seq, 2 * DIFF_D), lambda b, h, i: (b, q_off + hh + h)),
            pl.BlockSpec((seq, DIFF_DV), lambda b, h, i: (b, q_off + 2 * hh + h)),
            pl.BlockSpec((tq, DIFF_D), lambda b, h, i: (b * nq + i, 0)),
            pl.BlockSpec((tq, DIFF_D), lambda b, h, i: (b * nq + i, 0)),
            pl.BlockSpec((seq, DIFF_D), lambda b, h, i: (b, 0)),
            pl.BlockSpec((seq, DIFF_D), lambda b, h, i: (b, 0)),
            pl.BlockSpec((4, DIFF_D), lambda b, h, i: (0, 0)),
            pl.BlockSpec((1, DIFF_DV), lambda b, h, i: (0, 0)),
        ],
        out_specs=pl.BlockSpec((tq, DIFF_DV), lambda b, h, i: (b * nq + i, h)),
        out_shape=jax.ShapeDtypeStruct((batch * seq, hh * DIFF_DV), BF16),
        scratch_shapes=[pltpu.VMEM((2, seq, DIFF_D), BF16), pltpu.VMEM((seq, DIFF_DV), BF16)],
        compiler_params=_params("parallel", "parallel", "arbitrary"),
        name="diff_attention",
    )(p2, p2, p2, cos, sin_signed, cos, sin_signed, lam_params.astype(F32),
      subln_w.reshape(1, DIFF_DV).astype(F32))


def _rope_tables(positions, dim):
    inv = 1.0 / (ROPE_THETA ** (jnp.arange(0, dim, 2, dtype=F32) / dim))
    ang = positions.astype(F32)[..., None] * inv
    ang = jnp.concatenate([ang, ang], axis=-1).reshape(-1, dim)
    sign = jnp.concatenate([-jnp.ones((dim // 2,), F32), jnp.ones((dim // 2,), F32)])
    return jnp.cos(ang), jnp.sin(ang) * sign


def _ffn(x, h, ssq, w_gate, w_up, w_down, lead, d_model, d_ff, next_ln):
    sc = dict(ssq=ssq, scaled=(0, 1)) if ssq is not None else {}
    act = _matmul([(h, 0, d_model, w_gate, lead, 0, 0), (h, 0, d_model, w_up, lead, 0, 0)], [],
                  _ep_swiglu, d_ff, BF16, name="ffn_up", **sc)
    half = d_ff // 2
    x = _matmul([(act, 0, half, w_down, lead, 0, 0)], [(x, 0)],
                functools.partial(_ep_residual, 0.5), d_model, F32,
                name="ffn_down")
    return _matmul([(act, 1, half, w_down, lead, 1, 0)], [(x, 0)],
                   functools.partial(_ep_residual, 0.5), d_model, F32,
                   name="ffn_down_norm", next_ln=next_ln)


def kernel(x, p, positions, ln_ffn, ffn_w_gate, ffn_w_up, ffn_w_down, ln_mix, w_in, conv_w,
           gdn_a_log, gdn_dt_bias, gdn_norm_w, diff_lambda, diff_subln_w, w_branch, w_out,
           ln_ple, w_ple_gate, w_ple_proj, final_norm):
    batch, seq, d_model = x.shape
    depth = p.shape[0]
    ple_dim = p.shape[-1]
    d_ff = ffn_w_gate.shape[-1]
    t = batch * seq
    cos, sin_signed = _rope_tables(positions, RET_DK)

    n_ret = RET_HEADS * (2 * RET_DK + 2 * RET_DV)
    n_gdn = GDN_HEADS * (2 * GDN_DK + 2 * GDN_DV)
    n1 = n_ret + n_gdn
    n_ab = 4 * GDN_HEADS
    n_diff = DIFF_HEADS * (4 * DIFF_D + DIFF_DV)
    n2 = n_diff + N_BRANCH * d_model
    tn_in = TILES["w_in_tail"][1]

    x = x.reshape(t, d_model)
    p = p.reshape(depth, t, ple_dim)
    w_in_t = jnp.swapaxes(w_in, 1, 2)
    h, ssq = _rmsnorm(x, ln_ffn[0, 0], BF16), None
    for i in range(depth):
        lambda_init = 0.8 - 0.6 * math.exp(-0.3 * i)
        x, h, ssq = _ffn(x, h, ssq, ffn_w_gate, ffn_w_up, ffn_w_down, (i, 0), d_model, d_ff, ln_mix[i])

        sc = dict(ssq=ssq, scaled=(0,))
        p1 = _matmul([(h, 0, d_model, w_in_t, (i,), 0, 0)], [], _ep_identity, n1, F32,
                     name="w_in_head", transposed=True, **sc)
        pab = _matmul([(h, 0, d_model, w_in_t, (i,), 0, n1 // LANES)], [], _ep_identity, LANES, F32,
                      name="w_in_gates", transposed=True, **sc)
        p2 = _matmul([(h, 0, d_model, w_in_t, (i,), 0, n1 // tn_in)], [], _ep_identity, n2, BF16,
                     name="w_in_tail", transposed=True, row_shift=n_ab, **sc)

        y_ret = _retention(p1, cos, sin_signed, batch, seq)
        y_gdn = _gdn(p1, pab, conv_w[i], gdn_a_log[i], gdn_dt_bias[i], gdn_norm_w[i],
                     n_ret // LANES, (n_ret + GDN_HEADS * (2 * GDN_DK + GDN_DV)) // LANES,
                     batch, seq)
        y_diff = _diff_attention(p2, cos, sin_signed, diff_lambda[i], diff_subln_w[i],
                                 lambda_init, 0, batch, seq, tq=DIFF_TQ)

        bw = RET_HEADS * RET_DV
        tn_m = TILES["merge"][1]
        gate_off = n_diff // tn_m
        merged = _matmul(
            [(y, 0, bw, w_branch, (i, b), 0, 0) for b, y in enumerate((y_ret, y_gdn, y_diff))],
            [(p2, gate_off + b * (d_model // tn_m)) for b in range(N_BRANCH)],
            _ep_merge, d_model, BF16, name="merge")
        x, h, ssq = _matmul([(merged, 0, d_model, w_out, (i,), 0, 0)], [(x, 0)],
                            functools.partial(_ep_residual, 1.0), d_model, F32,
                            name="w_out", next_ln=ln_ffn[i, 1])

        x, h, ssq = _ffn(x, h, ssq, ffn_w_gate, ffn_w_up, ffn_w_down, (i, 1), d_model, d_ff, ln_ple[i])

        last = i + 1 == depth
        out = _matmul([(h, 0, d_model, w_ple_gate, (i,), 0, 0),
                       (p[i].astype(BF16), 0, ple_dim, w_ple_proj, (i,), 0, 0)], [(x, 0)],
                      _ep_ple, d_model, F32, name="ple", ssq=ssq, scaled=(0,),
                      next_ln=None if last else ln_ffn[i + 1, 0])
        x, h, ssq = (out, None, None) if last else out
    return _rmsnorm(x, final_norm, F32).reshape(batch, seq, d_model)
```
